```python
import math
import jax
import jax.numpy as jnp
from jax import lax
import numpy as np

D_MODEL = 1024
BATCH = 32
SEQ = 256
DEPTH = 4
DEC_BATCH = 8
DEC_SEQ = 4096
PAST_LEN = 256

GRID_W = 64
N_MIXERS = 3
N_GDN = (DEPTH + 2) // 3
N_DIFF = (DEPTH + 1) // 3
N_S5 = DEPTH // 3
GDN_HEADS = 8
GDN_DK = 128
GDN_DV = 128
GDN_CONV = 3
GDN_CHUNK = 64
DIFF_DH = 64
DIFF_HEADS = D_MODEL // (2 * DIFF_DH)
Q_BLOCK = 128
ROPE_BASE = 10000.0
S5_GROUP_CH = 16
S5_GROUPS = D_MODEL // S5_GROUP_CH
S5_STATE = 64
FFN_DIM = 2816
FFN_CONV = 3
MOD_CHUNKS = 6
EPS = 1e-6

kernel_name = 'hybrid_diffusion_gdn_diffattn_s5_step'


def rms_norm(x, gain):
    xf = x.astype(jnp.float32)
    y = xf * lax.rsqrt(jnp.mean(xf * xf, axis=-1, keepdims=True) + EPS)
    return (y * gain.astype(jnp.float32)).astype(x.dtype)


def l2_normalize(x):
    return x * lax.rsqrt(jnp.sum(x * x, axis=-1, keepdims=True) + EPS)


def depthwise_conv_centred(x, w):
    k, ch = w.shape
    return lax.conv_general_dilated(
        x, w[:, None, :].astype(x.dtype), window_strides=(1,), padding=[(k // 2, k // 2)],
        dimension_numbers=('NWC', 'WIO', 'NWC'), feature_group_count=ch)


def axial_rope_tables(n_tokens, dim):
    n_rows = n_tokens // GRID_W
    rows = jnp.repeat(jnp.arange(n_rows, dtype=jnp.float32), GRID_W)
    cols = jnp.tile(jnp.arange(GRID_W, dtype=jnp.float32), n_rows)
    n_freq = dim // 4
    inv_freq = ROPE_BASE ** (-jnp.arange(n_freq, dtype=jnp.float32) / n_freq)
    ang_r = rows[:, None] * inv_freq
    ang_c = cols[:, None] * inv_freq
    ang = jnp.concatenate([ang_r, ang_r, ang_c, ang_c], axis=-1)
    return jnp.cos(ang), jnp.sin(ang)


def rotate_half(z):
    z1, z2 = jnp.split(z, 2, axis=-1)
    return jnp.concatenate([-z2, z1], axis=-1)


def apply_axial_rope(x, cos, sin):
    xr, xc = jnp.split(x, 2, axis=-1)
    x_rot = jnp.concatenate([rotate_half(xr), rotate_half(xc)], axis=-1)
    cos = cos[None, :, None, None, :].astype(x.dtype)
    sin = sin[None, :, None, None, :].astype(x.dtype)
    return x * cos + x_rot * sin


def gdn_chunked(q, k, v, g, beta, s0):
    b, seq, h, dk = q.shape
    dv = v.shape[-1]
    c = GDN_CHUNK
    n = seq // c

    def to_chunks(t):
        t = t.reshape((b, n, c, h) + t.shape[3:])
        return jnp.moveaxis(t, (1, 3), (0, 2))

    q = to_chunks(q) * (dk ** -0.5)
    k = to_chunks(k)
    v = to_chunks(v)
    g = to_chunks(g)
    beta = to_chunks(beta)
    gc = jnp.cumsum(g, axis=-1)
    causal = jnp.tril(jnp.ones((c, c), dtype=bool))
    strict = jnp.tril(jnp.ones((c, c), dtype=bool), -1)
    gdiff = gc[..., :, None] - gc[..., None, :]
    decay = jnp.where(causal, jnp.exp(jnp.where(causal, gdiff, 0.0)), 0.0)
    kb = k * beta[..., None]
    lower = jnp.where(strict, jnp.einsum('nbhid,nbhjd->nbhij', kb, k) * decay, 0.0)
    rhs = jnp.concatenate([v * beta[..., None], kb * jnp.exp(gc)[..., None]], axis=-1)
    sol = lax.linalg.triangular_solve(lower + jnp.eye(c, dtype=q.dtype), rhs, left_side=True,
                                      lower=True, unit_diagonal=True)
    u, w = sol[..., :dv], sol[..., dv:]
    intra = jnp.where(causal, jnp.einsum('nbhid,nbhjd->nbhij', q, k) * decay, 0.0)
    q_dec = q * jnp.exp(gc)[..., None]
    k_dec = k * jnp.exp(gc[..., -1:] - gc)[..., None]
    g_last = jnp.exp(gc[..., -1])

    def step(s, xs):
        q_i, k_i, u_i, w_i, a_i, gl_i = xs
        v_new = u_i - jnp.einsum('bhcd,bhde->bhce', w_i, s)
        o_i = jnp.einsum('bhcd,bhde->bhce', q_i, s) + jnp.einsum('bhij,bhje->bhie', a_i, v_new)
        s = s * gl_i[..., None, None] + jnp.einsum('bhcd,bhce->bhde', k_i, v_new)
        return s, o_i

    s_fin, o = lax.scan(step, s0, (q_dec, k_dec, u, w, intra, g_last))
    o = jnp.moveaxis(o, (0, 2), (1, 3)).reshape(b, seq, h, dv)
    return o, s_fin


def gdn_mixer(h, w_qkv, conv_w, w_gate, w_alpha, w_beta, a_log, dt_bias, norm_g, w_out, cache):
    b, seq, _ = h.shape
    hk = GDN_HEADS * GDN_DK
    qkv = jax.nn.silu(depthwise_conv_centred(h @ w_qkv, conv_w)).astype(jnp.float32)
    q, k, v = jnp.split(qkv, [hk, 2 * hk], axis=-1)
    q = l2_normalize(q.reshape(b, seq, GDN_HEADS, GDN_DK))
    k = l2_normalize(k.reshape(b, seq, GDN_HEADS, GDN_DK))
    v = v.reshape(b, seq, GDN_HEADS, GDN_DV)
    alpha = (h @ w_alpha).astype(jnp.float32).reshape(b, seq, 2, GDN_HEADS)
    beta = jax.nn.sigmoid((h @ w_beta).astype(jnp.float32)).reshape(b, seq, 2, GDN_HEADS)
    g = -jnp.exp(a_log.astype(jnp.float32)) * jax.nn.softplus(alpha + dt_bias.astype(jnp.float32))
    if cache is None:
        s0 = jnp.zeros((b, 2, GDN_HEADS, GDN_DK, GDN_DV), jnp.float32)
    else:
        s0 = cache[0].astype(jnp.float32)
    o_f, s_f = gdn_chunked(q, k, v, g[:, :, 0], beta[:, :, 0], s0[:, 0])
    flip = lambda t: jnp.flip(t, axis=1)
    o_b, s_b = gdn_chunked(flip(q), flip(k), flip(v), flip(g[:, :, 1]), flip(beta[:, :, 1]), s0[:, 1])
    o = o_f + flip(o_b)
    gate = jax.nn.silu((h @ w_gate).astype(jnp.float32)).reshape(b, seq, GDN_HEADS, GDN_DV)
    o = rms_norm(o, norm_g) * gate
    y = o.reshape(b, seq, GDN_HEADS * GDN_DV).astype(h.dtype) @ w_out
    return y, (jnp.stack([s_f, s_b], axis=1),)


def diff_softmax_attend(q, k, v, lam):
    b, lq, h, _, dh = q.shape
    nb = lq // Q_BLOCK
    qb = jnp.moveaxis(q.reshape(b, nb, Q_BLOCK, h, 2, dh), 1, 0)
    scale = dh ** -0.5

    def one_block(qblk):
        s = jnp.einsum('bqhcd,bkhcd->bhcqk', qblk, k).astype(jnp.float32) * scale
        p = jax.nn.softmax(s, axis=-1)
        pd = p[:, :, 0] - lam * p[:, :, 1]
        return jnp.einsum('bhqk,bkhe->bqhe', pd.astype(v.dtype), v)

    o = lax.map(one_block, qb)
    return jnp.moveaxis(o, 0, 1).reshape(b, lq, h, 2 * dh)


def diff_attention_mixer(h, w_qkv, lam_vecs, subln, w_out, lam_init, cache):
    b, seq, _ = h.shape
    q, k, v = jnp.split(h @ w_qkv, 3, axis=-1)
    q = q.reshape(b, seq, DIFF_HEADS, 2, DIFF_DH)
    k = k.reshape(b, seq, DIFF_HEADS, 2, DIFF_DH)
    v = v.reshape(b, seq, DIFF_HEADS, 2 * DIFF_DH)
    if cache is None:
        k_all, v_all = k, v
    else:
        cos, sin = axial_rope_tables(seq, DIFF_DH)
        q = apply_axial_rope(q, cos, sin)
        k = apply_axial_rope(k, cos, sin)
        k_all = jnp.concatenate([cache[0].astype(k.dtype), k], axis=1)
        v_all = jnp.concatenate([cache[1].astype(v.dtype), v], axis=1)
    lf = lam_vecs.astype(jnp.float32)
    lam = jnp.exp(jnp.sum(lf[0] * lf[1])) - jnp.exp(jnp.sum(lf[2] * lf[3])) + lam_init
    o = diff_softmax_attend(q, k_all, v_all, lam)
    o = rms_norm(o, subln) * (1.0 - lam_init)
    return o.reshape(b, seq, D_MODEL) @ w_out, (k, v)


def complex_affine_combine(e1, e2):
    a1r, a1i, b1r, b1i = e1
    a2r, a2i, b2r, b2i = e2
    return (a2r * a1r - a2i * a1i, a2r * a1i + a2i * a1r,
            a2r * b1r - a2i * b1i + b2r, a2r * b1i + a2i * b1r + b2i)


def s5_scan(u, a_re, a_im, log_dt, b_re, b_im, c_re, c_im, h0_re, h0_im, reverse):
    seq = u.shape[1]
    dt = jnp.exp(log_dt)[:, None]
    mag = jnp.exp(dt * a_re)
    abar_re = mag * jnp.cos(dt * a_im)
    abar_im = mag * jnp.sin(dt * a_im)
    den = a_re * a_re + a_im * a_im
    f_re = ((abar_re - 1.0) * a_re + abar_im * a_im) / den
    f_im = (abar_im * a_re - (abar_re - 1.0) * a_im) / den
    bbar_re = f_re[..., None] * b_re - f_im[..., None] * b_im
    bbar_im = f_re[..., None] * b_im + f_im[..., None] * b_re
    if reverse:
        u = jnp.flip(u, axis=1)
    bu_re = jnp.einsum('blgc,gpc->blgp', u, bbar_re)
    bu_im = jnp.einsum('blgc,gpc->blgp', u, bbar_im)
    bu_re = bu_re.at[:, 0].add(abar_re * h0_re - abar_im * h0_im)
    bu_im = bu_im.at[:, 0].add(abar_re * h0_im + abar_im * h0_re)
    a_seq_re = jnp.broadcast_to(abar_re, (1, seq) + abar_re.shape)
    a_seq_im = jnp.broadcast_to(abar_im, (1, seq) + abar_im.shape)
    _, _, x_re, x_im = lax.associative_scan(complex_affine_combine,
                                            (a_seq_re, a_seq_im, bu_re, bu_im), axis=1)
    y = jnp.einsum('gcp,blgp->blgc', c_re, x_re) - jnp.einsum('gcp,blgp->blgc', c_im, x_im)
    if reverse:
        y = jnp.flip(y, axis=1)
    return y, x_re[:, -1], x_im[:, -1]


def s5_mixer(h, a_re, a_im, log_dt, b_re, b_im, c_re, c_im, d_skip, w_glu, cache):
    b, seq, _ = h.shape
    f = lambda t: t.astype(jnp.float32)
    hf = f(h)
    u = hf.reshape(b, seq, S5_GROUPS, S5_GROUP_CH)
    if cache is None:
        h0_re = jnp.zeros((b, 2, S5_GROUPS, S5_STATE), jnp.float32)
        h0_im = jnp.zeros((b, 2, S5_GROUPS, S5_STATE), jnp.float32)
    else:
        h0_re, h0_im = f(cache[0]), f(cache[1])
    y = f(d_skip) * hf
    fin_re, fin_im = [], []
    for d in range(2):
        yd, fr, fi = s5_scan(u, f(a_re[d]), f(a_im[d]), f(log_dt[d]), f(b_re[d]), f(b_im[d]),
                             f(c_re[d]), f(c_im[d]), h0_re[:, d], h0_im[:, d], d == 1)
        y = y + yd.reshape(b, seq, D_MODEL)
        fin_re.append(fr)
        fin_im.append(fi)
    z = jax.nn.gelu(y).astype(h.dtype) @ w_glu
    val, gate = jnp.split(z, 2, axis=-1)
    return val * jax.nn.sigmoid(gate), (jnp.stack(fin_re, axis=1), jnp.stack(fin_im, axis=1))


def conv_ffn(h, w_up, conv_w, w_down):
    u = depthwise_conv_centred(h @ w_up, conv_w)
    gate, val = jnp.split(u, 2, axis=-1)
    return (jax.nn.silu(gate) * val) @ w_down


def trunk_layer(p, l, x, cond, cache):
    kind, j = l % N_MIXERS, l // N_MIXERS
    mod = (jax.nn.silu(cond) @ p['w_mod'][l] + p['b_mod'][l])[:, None, :]
    shift1, scale1, gate1, shift2, scale2, gate2 = jnp.split(mod, MOD_CHUNKS, axis=-1)
    gains = p['norm_gain'][l]
    h = rms_norm(x, gains[0]) * (1.0 + scale1) + shift1
    if kind == 0:
        y, state = gdn_mixer(h, p['w_gdn_qkv'][j], p['gdn_conv'][j], p['w_gdn_gate'][j],
                             p['w_gdn_alpha'][j], p['w_gdn_beta'][j], p['gdn_a_log'][j],
                             p['gdn_dt_bias'][j], p['gdn_norm'][j], p['w_gdn_out'][j], cache)
    elif kind == 1:
        lam_init = 0.8 - 0.6 * math.exp(-0.3 * l)
        y, state = diff_attention_mixer(h, p['w_diff_qkv'][j], p['diff_lam'][j], p['diff_subln'][j],
                                        p['w_diff_out'][j], lam_init, cache)
    else:
        y, state = s5_mixer(h, p['s5_a_re'][j], p['s5_a_im'][j], p['s5_log_dt'][j], p['s5_b_re'][j],
                            p['s5_b_im'][j], p['s5_c_re'][j], p['s5_c_im'][j], p['s5_d'][j],
                            p['w_s5_glu'][j], cache)
    x = x + gate1 * rms_norm(y, gains[1])
    h = rms_norm(x, gains[2]) * (1.0 + scale2) + shift2
    x = x + gate2 * rms_norm(conv_ffn(h, p['w_ffn_up'][l], p['ffn_conv'][l], p['w_ffn_down'][l]), gains[3])
    return x, state


def setup_inputs(seed: int = 0) -> dict:
    key = jax.random.key(seed)
    keys = iter(jax.random.split(key, 48))
    f32 = jnp.float32

    def nrm(shape, scale):
        return jax.random.normal(next(keys), shape, f32) * scale

    def unif(shape, lo, hi):
        return jax.random.uniform(next(keys), shape, f32, lo, hi)

    hk = GDN_HEADS * GDN_DK
    hv = GDN_HEADS * GDN_DV
    dt_gdn = jnp.exp(unif((N_GDN, 2, GDN_HEADS), math.log(1e-3), math.log(1e-1)))
    s5_n = jnp.pi * jnp.arange(S5_STATE, dtype=f32)
    return {
        'x_prompt': nrm((BATCH, SEQ, D_MODEL), 1.0),
        'x_sample': nrm((DEC_BATCH, DEC_SEQ, D_MODEL), 1.0),
        'state_l0_gdn': nrm((DEC_BATCH, 2, GDN_HEADS, GDN_DK, GDN_DV), 0.1),
        'cache_l1_k': nrm((DEC_BATCH, PAST_LEN, DIFF_HEADS, 2, DIFF_DH), 1.0),
        'cache_l1_v': nrm((DEC_BATCH, PAST_LEN, DIFF_HEADS, 2 * DIFF_DH), 1.0),
        'state_l2_s5_re': nrm((DEC_BATCH, 2, S5_GROUPS, S5_STATE), 0.5),
        'state_l2_s5_im': nrm((DEC_BATCH, 2, S5_GROUPS, S5_STATE), 0.5),
        'state_l3_gdn': nrm((DEC_BATCH, 2, GDN_HEADS, GDN_DK, GDN_DV), 0.1),
        'c': nrm((DEC_BATCH, D_MODEL), 1.0),
        'c_ctx': nrm((D_MODEL,), 1.0),
        'w_mod': nrm((DEPTH, D_MODEL, MOD_CHUNKS * D_MODEL), D_MODEL ** -0.5),
        'b_mod': nrm((DEPTH, MOD_CHUNKS * D_MODEL), 0.01),
        'norm_gain': 1.0 + nrm((DEPTH, 4, D_MODEL), 0.02),
        'w_ffn_up': nrm((DEPTH, D_MODEL, 2 * FFN_DIM), D_MODEL ** -0.5),
        'ffn_conv': nrm((DEPTH, FFN_CONV, 2 * FFN_DIM), FFN_CONV ** -0.5),
        'w_ffn_down': nrm((DEPTH, FFN_DIM, D_MODEL), FFN_DIM ** -0.5),
        'w_gdn_qkv': nrm((N_GDN, D_MODEL, 2 * hk + hv), D_MODEL ** -0.5),
        'gdn_conv': nrm((N_GDN, GDN_CONV, 2 * hk + hv), GDN_CONV ** -0.5),
        'w_gdn_gate': nrm((N_GDN, D_MODEL, hv), D_MODEL ** -0.5),
        'w_gdn_alpha': nrm((N_GDN, D_MODEL, 2 * GDN_HEADS), D_MODEL ** -0.5),
        'w_gdn_beta': nrm((N_GDN, D_MODEL, 2 * GDN_HEADS), D_MODEL ** -0.5),
        'gdn_a_log': jnp.log(unif((N_GDN, 2, GDN_HEADS), 1.0, 16.0)),
        'gdn_dt_bias': dt_gdn + jnp.log(-jnp.expm1(-dt_gdn)),
        'gdn_norm': 1.0 + nrm((N_GDN, GDN_DV), 0.02),
        'w_gdn_out': nrm((N_GDN, hv, D_MODEL), hv ** -0.5),
        'w_diff_qkv': nrm((N_DIFF, D_MODEL, 3 * D_MODEL), D_MODEL ** -0.5),
        'diff_lam': nrm((N_DIFF, 4, DIFF_DH), 0.1),
        'diff_subln': 1.0 + nrm((N_DIFF, 2 * DIFF_DH), 0.02),
        'w_diff_out': nrm((N_DIFF, D_MODEL, D_MODEL), D_MODEL ** -0.5),
        's5_a_re': -0.5 + nrm((N_S5, 2, S5_GROUPS, S5_STATE), 0.01),
        's5_a_im': s5_n + nrm((N_S5, 2, S5_GROUPS, S5_STATE), 0.01),
        's5_log_dt': unif((N_S5, 2, S5_GROUPS), math.log(1e-3), math.log(1e-1)),
        's5_b_re': nrm((N_S5, 2, S5_GROUPS, S5_STATE, S5_GROUP_CH), (2 * S5_GROUP_CH) ** -0.5),
        's5_b_im': nrm((N_S5, 2, S5_GROUPS, S5_STATE, S5_GROUP_CH), (2 * S5_GROUP_CH) ** -0.5),
        's5_c_re': nrm((N_S5, 2, S5_GROUPS, S5_GROUP_CH, S5_STATE), S5_STATE ** -0.5),
        's5_c_im': nrm((N_S5, 2, S5_GROUPS, S5_GROUP_CH, S5_STATE), S5_STATE ** -0.5),
        's5_d': nrm((N_S5, D_MODEL), 1.0),
        'w_s5_glu': nrm((N_S5, D_MODEL, 2 * D_MODEL), D_MODEL ** -0.5),
    }


def reference(x_prompt, x_sample, state_l0_gdn, cache_l1_k, cache_l1_v, state_l2_s5_re, state_l2_s5_im,
              state_l3_gdn, c, c_ctx, w_mod, b_mod, norm_gain, w_ffn_up, ffn_conv, w_ffn_down,
              w_gdn_qkv, gdn_conv, w_gdn_gate, w_gdn_alpha, w_gdn_beta, gdn_a_log, gdn_dt_bias, gdn_norm,
              w_gdn_out, w_diff_qkv, diff_lam, diff_subln, w_diff_out, s5_a_re, s5_a_im, s5_log_dt,
              s5_b_re, s5_b_im, s5_c_re, s5_c_im, s5_d, w_s5_glu):
    p = dict(w_mod=w_mod, b_mod=b_mod, norm_gain=norm_gain, w_ffn_up=w_ffn_up, ffn_conv=ffn_conv,
             w_ffn_down=w_ffn_down, w_gdn_qkv=w_gdn_qkv, gdn_conv=gdn_conv, w_gdn_gate=w_gdn_gate,
             w_gdn_alpha=w_gdn_alpha, w_gdn_beta=w_gdn_beta, gdn_a_log=gdn_a_log, gdn_dt_bias=gdn_dt_bias,
             gdn_norm=gdn_norm, w_gdn_out=w_gdn_out, w_diff_qkv=w_diff_qkv, diff_lam=diff_lam,
             diff_subln=diff_subln, w_diff_out=w_diff_out, s5_a_re=s5_a_re, s5_a_im=s5_a_im,
             s5_log_dt=s5_log_dt, s5_b_re=s5_b_re, s5_b_im=s5_b_im, s5_c_re=s5_c_re, s5_c_im=s5_c_im,
             s5_d=s5_d, w_s5_glu=w_s5_glu)
    y_prompt = x_prompt
    ctx_states = []
    for l in range(DEPTH):
        y_prompt, st = trunk_layer(p, l, y_prompt, c_ctx[None, :], None)
        ctx_states.append(st)
    caches = [(state_l0_gdn,), (cache_l1_k, cache_l1_v), (state_l2_s5_re, state_l2_s5_im), (state_l3_gdn,)]
    y_sample = x_sample
    for l in range(DEPTH):
        y_sample, _ = trunk_layer(p, l, y_sample, c, caches[l])
    (st0,), (k1, v1), (s2_re, s2_im), (st3,) = ctx_states
    return (y_prompt, y_sample, st0, k1, v1, s2_re, s2_im, st3)
```

```python
import functools
import math

import jax
import jax.numpy as jnp
from jax import lax
from jax.experimental import pallas as pl
from jax.experimental.pallas import tpu as pltpu

F32 = jnp.float32
BF16 = jnp.bfloat16

D_MODEL = 1024
DEPTH = 4
GRID_W = 64
GDN_HEADS = 8
GDN_DK = 128
GDN_DV = 128
GDN_CHUNK = 64
DIFF_DH = 64
DIFF_HEADS = 8
ROPE_BASE = 10000.0
S5_GROUP_CH = 16
S5_GROUPS = 64
S5_STATE = 64
FFN_DIM = 2816
MOD_CHUNKS = 6
EPS = 1e-6

SUBLANES = 8
LANES = 128
VMEM_LIMIT_BYTES = 56 * 1024 * 1024

ROW_TILE = 1024
FFN_CHUNK = 256
S5_CHUNK = 16
ATT_KEY_CHUNK = 256


def _cparams(*sem):
    return pltpu.CompilerParams(dimension_semantics=sem, vmem_limit_bytes=VMEM_LIMIT_BYTES)


def _silu(x):
    return x * jax.nn.sigmoid(x)


def _rms(x, gain):
    return x * lax.rsqrt(jnp.mean(x * x, axis=-1, keepdims=True) + EPS) * gain


def _mod_slice(mod_ref, idx):
    return mod_ref[0, :, idx * D_MODEL:(idx + 1) * D_MODEL]


def _rms_mod(x, gain, mod_ref, shift_idx, scale_idx):
    return _rms(x, gain) * (1.0 + _mod_slice(mod_ref, scale_idx)) + _mod_slice(mod_ref, shift_idx)


def _residual(x, y, gain, mod_ref, gate_idx):
    return x + _mod_slice(mod_ref, gate_idx) * _rms(y, gain)


def _dot(a, b):
    return jnp.dot(a.astype(BF16), b.astype(BF16), preferred_element_type=F32)


def _dot_nt(a, b):
    return lax.dot_general(a.astype(BF16), b.astype(BF16), (((1,), (1,)), ((), ())),
                           preferred_element_type=F32)


def _dot_tn(a, b):
    return lax.dot_general(a.astype(BF16), b.astype(BF16), (((0,), (0,)), ((), ())),
                           preferred_element_type=F32)


def _dot_f32(a, b):
    return jnp.dot(a, b, preferred_element_type=F32, precision=lax.Precision.HIGHEST)


def _mod_kernel(c_ref, w_ref, b_ref, o_ref):
    o_ref[0] = _dot(_silu(c_ref[...]), w_ref[0]) + b_ref[0]


def _modulation(cond, w_mod, b_mod):
    rows = cond.shape[0]
    n = w_mod.shape[-1]
    tn = 1536
    return pl.pallas_call(
        _mod_kernel,
        grid=(DEPTH, n // tn),
        in_specs=[pl.BlockSpec((rows, D_MODEL), lambda l, j: (0, 0)),
                  pl.BlockSpec((1, D_MODEL, tn), lambda l, j: (l, 0, j)),
                  pl.BlockSpec((1, 1, tn), lambda l, j: (l, 0, j))],
        out_specs=pl.BlockSpec((1, rows, tn), lambda l, j: (l, 0, j)),
        out_shape=jax.ShapeDtypeStruct((DEPTH, rows, n), F32),
        compiler_params=_cparams("parallel", "parallel"),
        name="modulation",
    )(cond, w_mod, b_mod.reshape(DEPTH, 1, n))


def _norm_linear_kernel(x_ref, mod_ref, g_ref, w_ref, o_ref, h_scr):
    @pl.when(pl.program_id(1) == 0)
    def _():
        h_scr[...] = _rms_mod(x_ref[...], g_ref[...], mod_ref, 0, 1).astype(BF16)

    o_ref[...] = jnp.dot(h_scr[...], w_ref[...].astype(BF16), preferred_element_type=F32)


def _mod_spec(tm, rows_per_mod):
    return pl.BlockSpec((1, 1, MOD_CHUNKS * D_MODEL), lambda i, *_: ((i * tm) // rows_per_mod, 0, 0))


def _norm_linear(x, mod, gain, w, rows_per_mod, tn):
    t = x.shape[0]
    n = w.shape[1]
    tm = min(ROW_TILE, rows_per_mod)
    return pl.pallas_call(
        _norm_linear_kernel,
        grid=(t // tm, n // tn),
        in_specs=[pl.BlockSpec((tm, D_MODEL), lambda i, j: (i, 0)),
                  _mod_spec(tm, rows_per_mod),
                  pl.BlockSpec((1, D_MODEL), lambda i, j: (0, 0)),
                  pl.BlockSpec((D_MODEL, tn), lambda i, j: (0, j))],
        out_specs=pl.BlockSpec((tm, tn), lambda i, j: (i, j)),
        out_shape=jax.ShapeDtypeStruct((t, n), F32),
        scratch_shapes=[pltpu.VMEM((tm, D_MODEL), BF16)],
        compiler_params=_cparams("parallel", "arbitrary"),
        name="norm_linear",
    )(x, mod, gain, w)


def _norm_mod_kernel(x_ref, mod_ref, g_ref, o_ref):
    o_ref[...] = _rms_mod(x_ref[...], g_ref[...], mod_ref, 0, 1)


def _norm_mod(x, mod, gain, rows_per_mod):
    t = x.shape[0]
    tm = min(ROW_TILE, rows_per_mod)
    return pl.pallas_call(
        _norm_mod_kernel,
        grid=(t // tm,),
        in_specs=[pl.BlockSpec((tm, D_MODEL), lambda i: (i, 0)),
                  _mod_spec(tm, rows_per_mod),
                  pl.BlockSpec((1, D_MODEL), lambda i: (0, 0))],
        out_specs=pl.BlockSpec((tm, D_MODEL), lambda i: (i, 0)),
        out_shape=jax.ShapeDtypeStruct((t, D_MODEL), F32),
        compiler_params=_cparams("parallel"),
        name="norm_mod",
    )(x, mod, gain)


def _plain_out_kernel(x_ref, y_ref, mod_ref, g_ref, w_ref, o_ref):
    y = _dot(y_ref[...], w_ref[...])
    o_ref[...] = _residual(x_ref[...], y, g_ref[...], mod_ref, 2)


def _gdn_out_kernel(x_ref, of_ref, ob_ref, gate_ref, hn_ref, mod_ref, g_ref, w_ref, o_ref, y_scr):
    o = of_ref[...] + ob_ref[...]
    gate = _silu(gate_ref[...])
    hn = hn_ref[...]
    for h in range(GDN_HEADS):
        sl = slice(h * GDN_DV, (h + 1) * GDN_DV)
        y_scr[:, sl] = (_rms(o[:, sl], hn) * gate[:, sl]).astype(BF16)
    y = jnp.dot(y_scr[...], w_ref[...].astype(BF16), preferred_element_type=F32)
    o_ref[...] = _residual(x_ref[...], y, g_ref[...], mod_ref, 2)


def _s5_out_kernel(x_ref, h_ref, ys_ref, d_ref, mod_ref, g_ref, w_ref, o_ref):
    y = jax.nn.gelu(d_ref[...] * h_ref[...] + ys_ref[...])
    z = _dot(y, w_ref[...])
    val = z[:, :D_MODEL]
    gate = z[:, D_MODEL:]
    o_ref[...] = _residual(x_ref[...], val * jax.nn.sigmoid(gate), g_ref[...], mod_ref, 2)


def _row_spec(tm, width=D_MODEL, col=0):
    return pl.BlockSpec((tm, width), lambda i: (i, col))


def _const_spec(shape):
    return pl.BlockSpec(shape, lambda i: (0,) * len(shape))


def _plain_out(x, y, mod, gain, w, rows_per_mod):
    t = x.shape[0]
    tm = min(ROW_TILE, rows_per_mod)
    return pl.pallas_call(
        _plain_out_kernel,
        grid=(t // tm,),
        in_specs=[_row_spec(tm), _row_spec(tm), _mod_spec(tm, rows_per_mod),
                  _const_spec((1, D_MODEL)), _const_spec((D_MODEL, D_MODEL))],
        out_specs=_row_spec(tm),
        out_shape=jax.ShapeDtypeStruct((t, D_MODEL), F32),
        compiler_params=_cparams("parallel"),
        name="plain_out",
    )(x, y, mod, gain, w)


def _gdn_out(x, o_f, o_b, proj, head_gain, mod, gain, w, rows_per_mod):
    t = x.shape[0]
    tm = min(ROW_TILE, rows_per_mod)
    return pl.pallas_call(
        _gdn_out_kernel,
        grid=(t // tm,),
        in_specs=[_row_spec(tm), _row_spec(tm), _row_spec(tm),
                  _row_spec(tm, D_MODEL, 3),
                  _const_spec((1, GDN_DV)), _mod_spec(tm, rows_per_mod),
                  _const_spec((1, D_MODEL)), _const_spec((D_MODEL, D_MODEL))],
        out_specs=_row_spec(tm),
        out_shape=jax.ShapeDtypeStruct((t, D_MODEL), F32),
        scratch_shapes=[pltpu.VMEM((tm, D_MODEL), BF16)],
        compiler_params=_cparams("parallel"),
        name="gdn_out",
    )(x, o_f, o_b, proj, head_gain, mod, gain, w)


def _s5_out(x, h, y_scan, d_skip, mod, gain, w, rows_per_mod):
    t = x.shape[0]
    tm = min(ROW_TILE // 2, rows_per_mod)
    return pl.pallas_call(
        _s5_out_kernel,
        grid=(t // tm,),
        in_specs=[_row_spec(tm), _row_spec(tm), _row_spec(tm), _const_spec((1, D_MODEL)),
                  _mod_spec(tm, rows_per_mod), _const_spec((1, D_MODEL)),
                  _const_spec((D_MODEL, 2 * D_MODEL))],
        out_specs=_row_spec(tm),
        out_shape=jax.ShapeDtypeStruct((t, D_MODEL), F32),
        compiler_params=_cparams("parallel"),
        name="s5_out",
    )(x, h, y_scan, d_skip, mod, gain, w)


def _ffn_kernel(x_ref, xp_ref, xn_ref, mod_ref, g2_ref, g3_ref, wg_ref, wv_ref, cg_ref, cv_ref,
                wd_ref, o_ref, h_scr, z_scr, acc_scr, *, tm, seq_len):
    i = pl.program_id(0)
    k = pl.program_id(1)
    nk = pl.num_programs(1)
    ck = wg_ref.shape[1]

    @pl.when(k == 0)
    def _():
        g2 = g2_ref[...]
        h_scr[0:SUBLANES] = _rms_mod(xp_ref[...], g2, mod_ref, 3, 4).astype(BF16)
        h_scr[SUBLANES:SUBLANES + tm] = _rms_mod(x_ref[...], g2, mod_ref, 3, 4).astype(BF16)
        h_scr[SUBLANES + tm:] = _rms_mod(xn_ref[...], g2, mod_ref, 3, 4).astype(BF16)
        acc_scr[...] = jnp.zeros_like(acc_scr)

    h = h_scr[...]
    z_scr[:, :ck] = jnp.dot(h, wg_ref[...].astype(BF16), preferred_element_type=F32)
    z_scr[:, ck:] = jnp.dot(h, wv_ref[...].astype(BF16), preferred_element_type=F32)

    pos = (i * tm + lax.broadcasted_iota(jnp.int32, (tm, 1), 0)) % seq_len
    has_prev = pos != 0
    has_next = pos != seq_len - 1

    def conv(cols, cw_ref):
        cw = cw_ref[...]
        prev = jnp.where(has_prev, z_scr[SUBLANES - 1:SUBLANES - 1 + tm, cols], 0.0)
        mid = z_scr[SUBLANES:SUBLANES + tm, cols]
        nxt = jnp.where(has_next, z_scr[SUBLANES + 1:SUBLANES + 1 + tm, cols], 0.0)
        return cw[0:1] * prev + cw[1:2] * mid + cw[2:3] * nxt

    gate = conv(slice(0, ck), cg_ref)
    val = conv(slice(ck, 2 * ck), cv_ref)
    act = (_silu(gate) * val).astype(BF16)
    acc_scr[...] += jnp.dot(act, wd_ref[...].astype(BF16), preferred_element_type=F32)

    @pl.when(k == nk - 1)
    def _():
        o_ref[...] = _residual(x_ref[...], acc_scr[...], g3_ref[...], mod_ref, 5)


def _conv_ffn(x, mod, gain2, gain3, w_up, conv_w, w_down, seq_len, rows_per_mod):
    t = x.shape[0]
    tm = min(ROW_TILE, rows_per_mod)
    nk = FFN_DIM // FFN_CHUNK
    nblk8 = t // SUBLANES
    r8 = tm // SUBLANES
    kern = functools.partial(_ffn_kernel, tm=tm, seq_len=seq_len)
    return pl.pallas_call(
        kern,
        grid=(t // tm, nk),
        in_specs=[
            pl.BlockSpec((tm, D_MODEL), lambda i, k: (i, 0)),
            pl.BlockSpec((SUBLANES, D_MODEL), lambda i, k: (jnp.maximum(i * r8 - 1, 0), 0)),
            pl.BlockSpec((SUBLANES, D_MODEL), lambda i, k: (jnp.minimum((i + 1) * r8, nblk8 - 1), 0)),
            pl.BlockSpec((1, 1, MOD_CHUNKS * D_MODEL), lambda i, k: ((i * tm) // rows_per_mod, 0, 0)),
            pl.BlockSpec((1, D_MODEL), lambda i, k: (0, 0)),
            pl.BlockSpec((1, D_MODEL), lambda i, k: (0, 0)),
            pl.BlockSpec((D_MODEL, FFN_CHUNK), lambda i, k: (0, k)),
            pl.BlockSpec((D_MODEL, FFN_CHUNK), lambda i, k: (0, nk + k)),
            pl.BlockSpec((3, FFN_CHUNK), lambda i, k: (0, k)),
            pl.BlockSpec((3, FFN_CHUNK), lambda i, k: (0, nk + k)),
            pl.BlockSpec((FFN_CHUNK, D_MODEL), lambda i, k: (k, 0)),
        ],
        out_specs=pl.BlockSpec((tm, D_MODEL), lambda i, k: (i, 0)),
        out_shape=jax.ShapeDtypeStruct((t, D_MODEL), F32),
        scratch_shapes=[pltpu.VMEM((tm + 2 * SUBLANES, D_MODEL), BF16),
                        pltpu.VMEM((tm + 2 * SUBLANES, 2 * FFN_CHUNK), F32),
                        pltpu.VMEM((tm, D_MODEL), F32)],
        compiler_params=_cparams("parallel", "arbitrary"),
        name="conv_ffn",
    )(x, x, x, mod, gain2, gain3, w_up, w_up, conv_w, conv_w, w_down)


ROPE_HALF = DIFF_DH // 4
LOG2E = 1.4426950408889634


def _rope(x, cos, sin_signed):
    lane = lax.broadcasted_iota(jnp.int32, x.shape, 1)
    first = (lane % (2 * ROPE_HALF)) < ROPE_HALF
    rot = jnp.where(first, pltpu.roll(x, LANES - ROPE_HALF, 1), pltpu.roll(x, ROPE_HALF, 1))
    return x * cos + rot * sin_signed


def _diff_attn_body(q_ref, k_ref, v_ref, ck_ref, cv_ref, cos_ref, sin_ref, lam_ref, subln_ref,
                    o_ref, k_scr, vt_scr, *, n_cache, tq, lam_init):
    qi = pl.program_id(2)
    seq_len = k_ref.shape[0]
    n_keys = n_cache + seq_len
    rope = cos_ref is not None
    tr = 512 if seq_len % 512 == 0 else seq_len

    @pl.when(qi == 0)
    def _():
        k = k_ref[...]
        if rope:
            k = _rope(k, cos_ref[...], sin_ref[...])
        if n_cache:
            k_scr[0:n_cache] = ck_ref[0].astype(BF16)
            vt_scr[:, 0:n_cache] = cv_ref[0].T.astype(BF16)
        k_scr[n_cache:] = k.astype(BF16)
        for r in range(0, seq_len, tr):
            vt_scr[:, n_cache + r:n_cache + r + tr] = v_ref[r:r + tr, :].T.astype(BF16)

    q = q_ref[...]
    if rope:
        r0 = pl.multiple_of(qi * tq, tq)
        q = _rope(q, cos_ref[pl.ds(r0, tq), :], sin_ref[pl.ds(r0, tq), :])
    q = q * (DIFF_DH ** -0.5 * LOG2E)
    lane = lax.broadcasted_iota(jnp.int32, q.shape, 1)
    qz = (jnp.where(lane < DIFF_DH, q, 0.0).astype(BF16), jnp.where(lane >= DIFF_DH, q, 0.0).astype(BF16))

    m = [jnp.full((1, tq), -jnp.inf, F32) for _ in range(2)]
    l = [jnp.zeros((1, tq), F32) for _ in range(2)]
    acc = [jnp.zeros((2 * DIFF_DH, tq), F32) for _ in range(2)]
    kc = ATT_KEY_CHUNK
    for j in range(n_keys // kc):
        kch = k_scr[j * kc:(j + 1) * kc, :]
        vt = vt_scr[:, j * kc:(j + 1) * kc]
        for c in range(2):
            s = lax.dot_general(kch, qz[c], (((1,), (1,)), ((), ())), preferred_element_type=F32)
            m_new = jnp.maximum(m[c], jnp.max(s, axis=0, keepdims=True))
            p = jnp.exp2(s - m_new)
            alpha = jnp.exp2(m[c] - m_new)
            l[c] = alpha * l[c] + jnp.sum(p, axis=0, keepdims=True)
            acc[c] = alpha * acc[c] + jnp.dot(vt, p.astype(BF16), preferred_element_type=F32)
            m[c] = m_new

    lf = lam_ref[...]
    lam = (jnp.exp(jnp.sum(lf[0:1] * lf[1:2], axis=1, keepdims=True))
           - jnp.exp(jnp.sum(lf[2:3] * lf[3:4], axis=1, keepdims=True)) + lam_init)
    o = acc[0] * (1.0 / l[0]) - lam * (acc[1] * (1.0 / l[1]))
    ms = jnp.mean(o * o, axis=0, keepdims=True)
    y = (o * lax.rsqrt(ms + EPS) * subln_ref[...]) * (1.0 - lam_init)
    o_ref[...] = y.T


def _diff_attn_cached_kernel(q_ref, k_ref, v_ref, ck_ref, cv_ref, cos_ref, sin_ref, lam_ref, subln_ref,
                             o_ref, k_scr, vt_scr, **kw):
    _diff_attn_body(q_ref, k_ref, v_ref, ck_ref, cv_ref, cos_ref, sin_ref, lam_ref, subln_ref,
                    o_ref, k_scr, vt_scr, **kw)


def _diff_attn_context_kernel(q_ref, k_ref, v_ref, lam_ref, subln_ref, o_ref, k_scr, vt_scr, **kw):
    _diff_attn_body(q_ref, k_ref, v_ref, None, None, None, None, lam_ref, subln_ref,
                    o_ref, k_scr, vt_scr, **kw)


def _diff_attention(qkv, lam_vecs, subln, lam_init, nb, seq_len, cache=None, rope_tables=None):
    tq = 256
    nq = seq_len // tq
    width = 2 * DIFF_DH
    n_cache = 0 if cache is None else cache[0].shape[1]
    n_keys = n_cache + seq_len
    in_specs = [pl.BlockSpec((tq, width), lambda b, h, i: (b * nq + i, h)),
                pl.BlockSpec((seq_len, width), lambda b, h, i: (b, DIFF_HEADS + h)),
                pl.BlockSpec((seq_len, width), lambda b, h, i: (b, 2 * DIFF_HEADS + h))]
    args = [qkv, qkv, qkv]
    if cache is not None:
        in_specs += [pl.BlockSpec((1, n_cache, width), lambda b, h, i: (b, 0, h))] * 2
        in_specs += [pl.BlockSpec((seq_len, width), lambda b, h, i: (0, 0))] * 2
        args += [cache[0], cache[1], rope_tables[0], rope_tables[1]]
        kern = _diff_attn_cached_kernel
    else:
        kern = _diff_attn_context_kernel
    in_specs += [pl.BlockSpec((4, DIFF_DH), lambda b, h, i: (0, 0)),
                 pl.BlockSpec((width, 1), lambda b, h, i: (0, 0))]
    args += [lam_vecs, subln.reshape(width, 1)]
    return pl.pallas_call(
        functools.partial(kern, n_cache=n_cache, tq=tq, lam_init=lam_init),
        grid=(nb, DIFF_HEADS, nq),
        in_specs=in_specs,
        out_specs=pl.BlockSpec((tq, width), lambda b, h, i: (b * nq + i, h)),
        out_shape=jax.ShapeDtypeStruct((nb * seq_len, D_MODEL), F32),
        scratch_shapes=[pltpu.VMEM((n_keys, width), BF16), pltpu.VMEM((width, n_keys), BF16)],
        compiler_params=_cparams("parallel", "parallel", "arbitrary"),
        name="diff_attention",
    )(*args)


def _rope_tables(n_tokens):
    rows = (jnp.arange(n_tokens, dtype=jnp.int32) // GRID_W).astype(F32)
    cols = (jnp.arange(n_tokens, dtype=jnp.int32) % GRID_W).astype(F32)
    n_freq = DIFF_DH // 4
    inv_freq = ROPE_BASE ** (-jnp.arange(n_freq, dtype=F32) / n_freq)
    ang_r = rows[:, None] * inv_freq
    ang_c = cols[:, None] * inv_freq
    ang = jnp.concatenate([ang_r, ang_r, ang_c, ang_c] * 2, axis=-1)
    sign = jnp.where((jnp.arange(2 * DIFF_DH) % (2 * ROPE_HALF)) < ROPE_HALF, -1.0, 1.0).astype(F32)
    return jnp.cos(ang), jnp.sin(ang) * sign


GDN_QKV = 3 * GDN_HEADS * GDN_DK
GDN_PROJ = GDN_QKV + D_MODEL + LANES
GDN_EXT = GDN_CHUNK + 2 * SUBLANES


def _cumsum_rows(x, reverse):
    n = x.shape[0]
    row = lax.broadcasted_iota(jnp.int32, x.shape, 0)
    k = 1
    while k < n:
        if reverse:
            x = x + jnp.where(row < n - k, pltpu.roll(x, n - k, 0), 0.0)
        else:
            x = x + jnp.where(row >= k, pltpu.roll(x, k, 0), 0.0)
        k *= 2
    return x


def _gdn_unit(q, k, v, gcol, grow, bcol, gtot, s, incl, strict, eye, blocks):
    gdiff = gcol - grow
    decay = jnp.where(incl, jnp.exp(jnp.where(incl, gdiff, 0.0)), 0.0)
    kb = k * bcol
    lower = jnp.where(strict, _dot_nt(kb, k) * decay, 0.0)
    intra = _dot_nt(q, k) * decay
    t_off = -jnp.where(blocks[0], lower, 0.0)
    for blk in blocks[1:]:
        t_inv = eye + t_off
        t_off = t_off - _dot(_dot(t_inv, jnp.where(blk, lower, 0.0)), t_inv)
    egc = jnp.exp(gcol)
    rhs = jnp.concatenate([v * bcol, kb * egc], axis=1)
    sol = rhs + _dot(t_off, rhs)
    u = sol[:, :GDN_DV]
    w = sol[:, GDN_DV:]
    v_new = u - _dot(w, s)
    o = _dot(q * egc, s) + _dot(intra, v_new)
    s_new = s * jnp.exp(gtot) + _dot_tn(k * jnp.exp(gtot - gcol), v_new)
    return o, s_new


def _gdn_kernel(fm_ref, fp_ref, fn_ref, fab_ref, bm_ref, bp_ref, bn_ref, bab_ref, cw_ref, alog_ref,
                dtb_ref, *rest, has_s0):
    if has_s0:
        s0_ref, of_ref, ob_ref, s_ref, ext_scr = rest
    else:
        of_ref, ob_ref, s_ref, ext_scr = rest
    c = pl.program_id(1)
    n = pl.num_programs(1)

    @pl.when(c == 0)
    def _():
        if has_s0:
            s_ref[...] = s0_ref[...]
        else:
            s_ref[...] = jnp.zeros_like(s_ref)

    ri = lax.broadcasted_iota(jnp.int32, (GDN_CHUNK, GDN_CHUNK), 0)
    ci = lax.broadcasted_iota(jnp.int32, (GDN_CHUNK, GDN_CHUNK), 1)
    eye = jnp.where(ri == ci, 1.0, 0.0)
    blocks = []
    size = 1
    while size < GDN_CHUNK:
        blocks.append((ri // (2 * size) == ci // (2 * size)) & (ri // size != ci // size))
        size *= 2
    cw = cw_ref[...]
    neg_a = -jnp.exp(alog_ref[...])
    dtb = dtb_ref[...]

    for d, (m_ref, p_ref, n_ref, ab_ref, o_ref) in enumerate(
            ((fm_ref, fp_ref, fn_ref, fab_ref, of_ref), (bm_ref, bp_ref, bn_ref, bab_ref, ob_ref))):
        cc = c if d == 0 else n - 1 - c
        ext_scr[0:SUBLANES] = p_ref[...] * jnp.where(cc == 0, 0.0, 1.0)
        ext_scr[SUBLANES:SUBLANES + GDN_CHUNK] = m_ref[...]
        ext_scr[SUBLANES + GDN_CHUNK:] = n_ref[...] * jnp.where(cc == n - 1, 0.0, 1.0)
        conv = (cw[0:1] * ext_scr[SUBLANES - 1:SUBLANES - 1 + GDN_CHUNK]
                + cw[1:2] * m_ref[...]
                + cw[2:3] * ext_scr[SUBLANES + 1:SUBLANES + 1 + GDN_CHUNK])
        qkv = _silu(conv)

        ab = ab_ref[...]
        x = ab + dtb
        g = neg_a * (jnp.maximum(x, 0.0) + jnp.log(1.0 + jnp.exp(-jnp.abs(x))))
        beta = jax.nn.sigmoid(ab)
        gc = _cumsum_rows(g, reverse=(d == 1))
        gct = jnp.concatenate([gc, gc], axis=0).T
        if d == 0:
            incl, strict, last = ri >= ci, ri > ci, GDN_CHUNK - 1
        else:
            incl, strict, last = ri <= ci, ri < ci, 0

        for h in range(GDN_HEADS):
            col = d * GDN_HEADS + h
            q = qkv[:, h * GDN_DK:(h + 1) * GDN_DK]
            k = qkv[:, (GDN_HEADS + h) * GDN_DK:(GDN_HEADS + h + 1) * GDN_DK]
            v = qkv[:, (2 * GDN_HEADS + h) * GDN_DK:(2 * GDN_HEADS + h + 1) * GDN_DK]
            q = q * lax.rsqrt(jnp.sum(q * q, axis=-1, keepdims=True) + EPS) * (GDN_DK ** -0.5)
            k = k * lax.rsqrt(jnp.sum(k * k, axis=-1, keepdims=True) + EPS)
            gcol = gc[:, col:col + 1]
            grow = gct[col:col + 1, 0:GDN_CHUNK]
            bcol = beta[:, 2 * GDN_HEADS + col:2 * GDN_HEADS + col + 1]
            gtot = gc[last:last + 1, col:col + 1]
            o, s_new = _gdn_unit(q, k, v, gcol, grow, bcol, gtot, s_ref[0, d, h], incl, strict, eye, blocks)
            o_ref[:, h * GDN_DV:(h + 1) * GDN_DV] = o
            s_ref[0, d, h] = s_new


def _gdn_scan(proj, conv_w, a_log, dt_bias, nb, seq_len, s0=None):
    n = seq_len // GDN_CHUNK
    t = nb * seq_len
    r8 = GDN_CHUNK // SUBLANES
    nblk8 = t // SUBLANES
    ab_col = (GDN_QKV + D_MODEL) // LANES
    pad = lambda p: jnp.pad(p.reshape(1, 2 * GDN_HEADS), ((0, 0), (0, LANES - 2 * GDN_HEADS)))

    def chunk_specs(chunk_of):
        return [pl.BlockSpec((GDN_CHUNK, GDN_QKV), lambda b, c: (b * n + chunk_of(c), 0)),
                pl.BlockSpec((SUBLANES, GDN_QKV), lambda b, c: (jnp.maximum((b * n + chunk_of(c)) * r8 - 1, 0), 0)),
                pl.BlockSpec((SUBLANES, GDN_QKV),
                             lambda b, c: (jnp.minimum((b * n + chunk_of(c) + 1) * r8, nblk8 - 1), 0)),
                pl.BlockSpec((GDN_CHUNK, LANES), lambda b, c: (b * n + chunk_of(c), ab_col))]

    state_spec = pl.BlockSpec((1, 2, GDN_HEADS, GDN_DK, GDN_DV), lambda b, c: (b, 0, 0, 0, 0))
    in_specs = (chunk_specs(lambda c: c) + chunk_specs(lambda c: n - 1 - c)
                + [pl.BlockSpec((3, GDN_QKV), lambda b, c: (0, 0)),
                   pl.BlockSpec((1, LANES), lambda b, c: (0, 0)),
                   pl.BlockSpec((1, LANES), lambda b, c: (0, 0))])
    args = [proj] * 8 + [conv_w, pad(a_log), pad(dt_bias)]
    if s0 is not None:
        in_specs.append(state_spec)
        args.append(s0)
    hv = GDN_HEADS * GDN_DV
    return pl.pallas_call(
        functools.partial(_gdn_kernel, has_s0=s0 is not None),
        grid=(nb, n),
        in_specs=in_specs,
        out_specs=[pl.BlockSpec((GDN_CHUNK, hv), lambda b, c: (b * n + c, 0)),
                   pl.BlockSpec((GDN_CHUNK, hv), lambda b, c: (b * n + n - 1 - c, 0)),
                   state_spec],
        out_shape=[jax.ShapeDtypeStruct((t, hv), F32), jax.ShapeDtypeStruct((t, hv), F32),
                   jax.ShapeDtypeStruct((nb, 2, GDN_HEADS, GDN_DK, GDN_DV), F32)],
        scratch_shapes=[pltpu.VMEM((GDN_EXT, GDN_QKV), F32)],
        compiler_params=_cparams("parallel", "arbitrary"),
        name="gdn_scan",
    )(*args)


def _gdn_mixer(x, mod, gains, w_cat, conv_w, a_log, dt_bias, head_gain, w_out, nb, seq_len, rows_per_mod,
               s0=None):
    proj = _norm_linear(x, mod, gains[0][None], w_cat, rows_per_mod, LANES * 11)
    o_f, o_b, s_fin = _gdn_scan(proj, conv_w, a_log, dt_bias, nb, seq_len, s0)
    x = _gdn_out(x, o_f, o_b, proj, head_gain[None], mod, gains[1][None], w_out, rows_per_mod)
    return x, s_fin


def _diff_mixer(x, mod, gains, w_qkv, lam_vecs, subln, w_out, lam_init, nb, seq_len, rows_per_mod,
                cache=None, rope_tables=None):
    qkv = _norm_linear(x, mod, gains[0][None], w_qkv, rows_per_mod, D_MODEL)
    heads = _diff_attention(qkv, lam_vecs, subln, lam_init, nb, seq_len, cache, rope_tables)
    x = _plain_out(x, heads, mod, gains[1][None], w_out, rows_per_mod)
    return x, qkv


S5_IN = S5_CHUNK * S5_GROUP_CH
S5_BITS = 5


def _cmul(ar, ai, br, bi):
    return ar * br - ai * bi, ar * bi + ai * br


def _discretize(a_re, a_im, log_dt):
    dt = jnp.exp(log_dt)
    mag = jnp.exp(dt * a_re)
    abar_re = mag * jnp.cos(dt * a_im)
    abar_im = mag * jnp.sin(dt * a_im)
    den = a_re * a_re + a_im * a_im
    f_re = ((abar_re - 1.0) * a_re + abar_im * a_im) / den
    f_im = (abar_im * a_re - (abar_re - 1.0) * a_im) / den
    return abar_re, abar_im, f_re, f_im


def _cpow(ar, ai, expo):
    pr = jnp.ones(expo.shape, F32)
    pi = jnp.zeros(expo.shape, F32)
    for b in range(S5_BITS):
        bit = ((expo >> b) & 1) == 1
        pr, pi = _cmul(pr, pi, jnp.where(bit, ar, 1.0), jnp.where(bit, ai, 0.0))
        if b + 1 < S5_BITS:
            ar, ai = _cmul(ar, ai, ar, ai)
    return pr, pi


def _s5_ops_kernel(arow_ref, acol_ref, bpc_ref, bt_ref, ccat_ref, csep_ref, mc_ref, ef_ref, a16_ref):
    arow = arow_ref[0]
    ar, ai, fr, fi = _discretize(arow[0:1], arow[1:2], arow[2:3])
    row = lax.broadcasted_iota(jnp.int32, (S5_IN, LANES), 0)
    lane = lax.broadcasted_iota(jnp.int32, (S5_IN, LANES), 1)
    tok = row // S5_GROUP_CH
    fwd = lane < S5_STATE
    tile = lambda x: jnp.concatenate([x] * S5_CHUNK, axis=0)

    pr, pi = _cpow(ar, ai, jnp.where(fwd, S5_CHUNK - 1 - tok, tok))
    bt = bt_ref[0]
    bbr, bbi = _cmul(fr, fi, bt[0], bt[1])
    er, ei = _cmul(pr, pi, tile(bbr), tile(bbi))
    ef_ref[0, 0] = er
    ef_ref[0, 1] = ei
    pr, pi = _cpow(ar, ai, jnp.where(fwd, tok + 1, S5_CHUNK - tok))
    cc = ccat_ref[0]
    gr, gi = _cmul(pr, pi, tile(cc[0]), tile(cc[1]))
    ef_ref[0, 2] = gr
    ef_ref[0, 3] = -gi
    p16r, p16i = _cpow(ar, ai, jnp.full((1, LANES), S5_CHUNK, jnp.int32))
    a16_ref[0, 0:1] = p16r
    a16_ref[0, 1:2] = p16i

    acol = acol_ref[0]
    lane_w = lax.broadcasted_iota(jnp.int32, (S5_GROUP_CH, S5_IN), 1)
    place = jnp.where(lane_w % S5_GROUP_CH == lax.broadcasted_iota(jnp.int32, (S5_GROUP_CH, S5_IN), 0), 1.0, 0.0)
    lag = lax.broadcasted_iota(jnp.int32, (S5_STATE, S5_IN), 1) // S5_GROUP_CH
    kw = []
    for d in range(2):
        cr, ci, gr_, gi_ = _discretize(acol[:, 3 * d:3 * d + 1], acol[:, 3 * d + 1:3 * d + 2],
                                       acol[:, 3 * d + 2:3 * d + 3])
        bbr, bbi = _cmul(gr_, gi_, bpc_ref[0, 2 * d], bpc_ref[0, 2 * d + 1])
        pr, pi = _cpow(jnp.broadcast_to(cr, lag.shape), jnp.broadcast_to(ci, lag.shape),
                       S5_CHUNK - 1 - lag if d == 0 else lag)
        hr, hi = _cmul(pr, pi, _dot_f32(bbr, place), _dot_f32(bbi, place))
        kw.append(_dot_f32(csep_ref[0, 2 * d], hr) - _dot_f32(csep_ref[0, 2 * d + 1], hi))
    for t in range(S5_CHUNK):
        sh = (S5_CHUNK - 1 - t) * S5_GROUP_CH
        m_f = jnp.where(lane_w < (t + 1) * S5_GROUP_CH, pltpu.roll(kw[0], (S5_IN - sh) % S5_IN, 1), 0.0)
        m_b = jnp.where(lane_w >= t * S5_GROUP_CH, pltpu.roll(kw[1], t * S5_GROUP_CH, 1), 0.0)
        mc_ref[0, t * S5_GROUP_CH:(t + 1) * S5_GROUP_CH, :] = m_f + m_b


def _s5_operators(a_re, a_im, log_dt, b_re, b_im, c_re, c_im):
    g = S5_GROUPS
    cat = lambda x: jnp.concatenate([x[0], x[1]], axis=-1)
    dt_row = jnp.broadcast_to(log_dt[:, :, None], (2, g, S5_STATE))
    arow = jnp.stack([cat(a_re), cat(a_im), cat(dt_row)], axis=1)
    acol = jnp.stack([a_re[0], a_im[0], dt_row[0], a_re[1], a_im[1], dt_row[1]], axis=-1)
    bpc = jnp.stack([b_re[0], b_im[0], b_re[1], b_im[1]], axis=1)
    swap = lambda x: jnp.swapaxes(x, -1, -2)
    bt = jnp.stack([cat(swap(b_re)), cat(swap(b_im))], axis=1)
    ccat = jnp.stack([cat(c_re), cat(c_im)], axis=1)
    csep = jnp.stack([c_re[0], c_im[0], c_re[1], c_im[1]], axis=1)
    spec = lambda *s: pl.BlockSpec((1,) + s, lambda i: (i,) + (0,) * len(s))
    return pl.pallas_call(
        _s5_ops_kernel,
        grid=(g,),
        in_specs=[spec(3, LANES), spec(S5_STATE, 6), spec(4, S5_STATE, S5_GROUP_CH),
                  spec(2, S5_GROUP_CH, LANES), spec(2, S5_GROUP_CH, LANES), spec(4, S5_GROUP_CH, S5_STATE)],
        out_specs=[spec(S5_IN, S5_IN), spec(4, S5_IN, LANES), spec(2, LANES)],
        out_shape=[jax.ShapeDtypeStruct((g, S5_IN, S5_IN), F32),
                   jax.ShapeDtypeStruct((g, 4, S5_IN, LANES), F32),
                   jax.ShapeDtypeStruct((g, 2, LANES), F32)],
        compiler_params=_cparams("parallel"),
        name="s5_operators",
    )(arow, acol, bpc, bt, ccat, csep)


def _s5_scan_kernel(u_ref, mc_ref, ef_ref, a16_ref, *rest, nb, has_x0):
    if has_x0:
        x0r_ref, x0i_ref, y_ref, fr_ref, fi_ref, xr_scr, xi_scr = rest
    else:
        y_ref, fr_ref, fi_ref, xr_scr, xi_scr = rest
    u = u_ref[0].astype(BF16)
    rows = u.shape[0]
    nc = rows // nb
    vr = jnp.dot(u, ef_ref[0, 0].astype(BF16), preferred_element_type=F32)
    vi = jnp.dot(u, ef_ref[0, 1].astype(BF16), preferred_element_type=F32)
    row_c = lax.broadcasted_iota(jnp.int32, (rows, LANES), 0) % nc
    fwd = lax.broadcasted_iota(jnp.int32, (rows, LANES), 1) < S5_STATE
    pr = a16_ref[0, 0:1]
    pi = a16_ref[0, 1:2]
    if has_x0:
        expand = lambda x: jnp.broadcast_to(x[:, None, :], (nb, nc, LANES)).reshape(rows, LANES)
        x0r, x0i = expand(x0r_ref[0]), expand(x0i_ref[0])
        first = row_c == jnp.where(fwd, 0, nc - 1)
        ir, ii = _cmul(pr, pi, x0r, x0i)
        vr = vr + jnp.where(first, ir, 0.0)
        vi = vi + jnp.where(first, ii, 0.0)

    def shifted(x, k):
        down = jnp.where(row_c >= k, pltpu.roll(x, k, 0), 0.0)
        up = jnp.where(row_c < nc - k, pltpu.roll(x, rows - k, 0), 0.0)
        return jnp.where(fwd, down, up)

    k = 1
    while k < nc:
        sr, si = _cmul(pr, pi, shifted(vr, k), shifted(vi, k))
        vr, vi = vr + sr, vi + si
        pr, pi = _cmul(pr, pi, pr, pi)
        k *= 2
    er, ei = shifted(vr, 1), shifted(vi, 1)
    if has_x0:
        er = jnp.where(first, x0r, er)
        ei = jnp.where(first, x0i, ei)
    y = lax.dot_general(u, mc_ref[0].astype(BF16), (((1,), (1,)), ((), ())), preferred_element_type=F32)
    y = y + _dot_nt(er, ef_ref[0, 2]) + _dot_nt(ei, ef_ref[0, 3])
    y_ref[0] = y
    xr_scr[...] = vr
    xi_scr[...] = vi
    lane_b = lax.broadcasted_iota(jnp.int32, (nb, LANES), 1) < S5_STATE
    fr_ref[0] = jnp.where(lane_b, xr_scr[pl.ds(nc - 1, nb, stride=nc), :], xr_scr[pl.ds(0, nb, stride=nc), :])
    fi_ref[0] = jnp.where(lane_b, xi_scr[pl.ds(nc - 1, nb, stride=nc), :], xi_scr[pl.ds(0, nb, stride=nc), :])


def _s5_scan(u, ops, nb, x0=None):
    mc, ef, a16 = ops
    g, rows, _ = u.shape
    spec = lambda *s: pl.BlockSpec((1,) + s, lambda i: (i,) + (0,) * len(s))
    in_specs = [spec(rows, S5_IN), spec(S5_IN, S5_IN), spec(4, S5_IN, LANES), spec(2, LANES)]
    args = [u, mc, ef, a16]
    if x0 is not None:
        in_specs += [spec(nb, LANES), spec(nb, LANES)]
        args += list(x0)
    return pl.pallas_call(
        functools.partial(_s5_scan_kernel, nb=nb, has_x0=x0 is not None),
        grid=(g,),
        in_specs=in_specs,
        out_specs=[spec(rows, S5_IN), spec(nb, LANES), spec(nb, LANES)],
        out_shape=[jax.ShapeDtypeStruct((g, rows, S5_IN), F32),
                   jax.ShapeDtypeStruct((g, nb, LANES), F32),
                   jax.ShapeDtypeStruct((g, nb, LANES), F32)],
        scratch_shapes=[pltpu.VMEM((rows, LANES), F32), pltpu.VMEM((rows, LANES), F32)],
        compiler_params=_cparams("parallel"),
        name="s5_scan",
    )(*args)


def _to_chunk_major(h, nb, seq_len):
    nc = seq_len // S5_CHUNK
    h = h.reshape(nb * nc, S5_CHUNK, S5_GROUPS, S5_GROUP_CH)
    return jnp.transpose(h, (2, 0, 1, 3)).reshape(S5_GROUPS, nb * nc, S5_IN)


def _from_chunk_major(y, nb, seq_len):
    nc = seq_len // S5_CHUNK
    y = y.reshape(S5_GROUPS, nb * nc, S5_CHUNK, S5_GROUP_CH)
    return jnp.transpose(y, (1, 2, 0, 3)).reshape(nb * seq_len, D_MODEL)


def _s5_mixer(x, mod, gains, ops, d_skip, w_glu, nb, seq_len, rows_per_mod, cache=None):
    h = _norm_mod(x, mod, gains[0][None], rows_per_mod)
    x0 = None
    if cache is not None:
        x0 = [jnp.transpose(s, (2, 0, 1, 3)).reshape(S5_GROUPS, nb, 2 * S5_STATE) for s in cache]
    y, fin_re, fin_im = _s5_scan(_to_chunk_major(h, nb, seq_len), ops, nb, x0)
    x = _s5_out(x, h, _from_chunk_major(y, nb, seq_len), d_skip[None], mod, gains[1][None], w_glu, rows_per_mod)
    unpack = lambda f: jnp.transpose(f.reshape(S5_GROUPS, nb, 2, S5_STATE), (1, 2, 0, 3))
    return x, (unpack(fin_re), unpack(fin_im))


def kernel(x_prompt, x_sample, state_l0_gdn, cache_l1_k, cache_l1_v, state_l2_s5_re, state_l2_s5_im, state_l3_gdn, c, c_ctx, w_mod, b_mod, norm_gain, w_ffn_up, ffn_conv, w_ffn_down, w_gdn_qkv, gdn_conv, w_gdn_gate, w_gdn_alpha, w_gdn_beta, gdn_a_log, gdn_dt_bias, gdn_norm, w_gdn_out, w_diff_qkv, diff_lam, diff_subln, w_diff_out, s5_a_re, s5_a_im, s5_log_dt, s5_b_re, s5_b_im, s5_c_re, s5_c_im, s5_d, w_s5_glu):
    nbp, lp = x_prompt.shape[:2]
    nbs, ls = x_sample.shape[:2]
    tp, ts = nbp * lp, nbs * ls
    pad_rows = -(nbs + 1) % SUBLANES
    cond = jnp.concatenate([c, c_ctx[None], jnp.zeros((pad_rows, D_MODEL), F32)], axis=0)
    mod_all = _modulation(cond, w_mod, b_mod)
    s5_ops = _s5_operators(s5_a_re[0], s5_a_im[0], s5_log_dt[0], s5_b_re[0], s5_b_im[0], s5_c_re[0], s5_c_im[0])
    rope = _rope_tables(ls)
    past = cache_l1_k.shape[1]
    attn_cache = (cache_l1_k.reshape(nbs, past, D_MODEL), cache_l1_v.reshape(nbs, past, D_MODEL))
    gdn_caches = (state_l0_gdn, state_l3_gdn)

    xp = x_prompt.reshape(tp, D_MODEL)
    xs = x_sample.reshape(ts, D_MODEL)
    ctx_states = []
    for l in range(DEPTH):
        kind, j = l % 3, l // 3
        gains = norm_gain[l]
        mod_s = mod_all[l, :nbs][:, None, :]
        mod_p = mod_all[l, nbs:nbs + 1][:, None, :]
        if kind == 0:
            w_cat = jnp.concatenate([w_gdn_qkv[j], w_gdn_gate[j], w_gdn_alpha[j], w_gdn_beta[j],
                                     jnp.zeros((D_MODEL, LANES - 4 * GDN_HEADS), F32)], axis=1)
            gdn = functools.partial(_gdn_mixer, gains=gains, w_cat=w_cat, conv_w=gdn_conv[j], a_log=gdn_a_log[j],
                                    dt_bias=gdn_dt_bias[j], head_gain=gdn_norm[j], w_out=w_gdn_out[j])
            xp, st = gdn(xp, mod_p, nb=nbp, seq_len=lp, rows_per_mod=tp)
            xs, _ = gdn(xs, mod_s, nb=nbs, seq_len=ls, rows_per_mod=ls, s0=gdn_caches[j])
            ctx_states.append(st)
        elif kind == 1:
            lam_init = 0.8 - 0.6 * math.exp(-0.3 * l)
            attn = functools.partial(_diff_mixer, gains=gains, w_qkv=w_diff_qkv[j], lam_vecs=diff_lam[j],
                                     subln=diff_subln[j], w_out=w_diff_out[j], lam_init=lam_init)
            xp, qkv = attn(xp, mod_p, nb=nbp, seq_len=lp, rows_per_mod=tp)
            xs, _ = attn(xs, mod_s, nb=nbs, seq_len=ls, rows_per_mod=ls, cache=attn_cache, rope_tables=rope)
            ctx_states.append((qkv[:, D_MODEL:2 * D_MODEL].reshape(nbp, lp, DIFF_HEADS, 2, DIFF_DH),
                               qkv[:, 2 * D_MODEL:].reshape(nbp, lp, DIFF_HEADS, 2 * DIFF_DH)))
        else:
            s5 = functools.partial(_s5_mixer, gains=gains, ops=s5_ops, d_skip=s5_d[j], w_glu=w_s5_glu[j])
            xp, st = s5(xp, mod_p, nb=nbp, seq_len=lp, rows_per_mod=tp)
            xs, _ = s5(xs, mod_s, nb=nbs, seq_len=ls, rows_per_mod=ls, cache=(state_l2_s5_re, state_l2_s5_im))
            ctx_states.append(st)
        ffn = functools.partial(_conv_ffn, gain2=gains[2][None], gain3=gains[3][None], w_up=w_ffn_up[l],
                                conv_w=ffn_conv[l], w_down=w_ffn_down[l])
        xp = ffn(xp, mod_p, seq_len=lp, rows_per_mod=tp)
        xs = ffn(xs, mod_s, seq_len=ls, rows_per_mod=ls)
    st0, (k1, v1), (s2_re, s2_im), st3 = ctx_states
    return (xp.reshape(nbp, lp, D_MODEL), xs.reshape(nbs, ls, D_MODEL), st0, k1, v1, s2_re, s2_im, st3)
```

```python
import functools
import math

import jax
import jax.numpy as jnp
from jax import lax
from jax.experimental import pallas as pl
from jax.experimental.pallas import tpu as pltpu

F32 = jnp.float32
BF16 = jnp.bfloat16

D_MODEL = 1024
DEPTH = 4
GRID_W = 64
GDN_HEADS = 8
GDN_DK = 128
GDN_DV = 128
GDN_CHUNK = 64
DIFF_DH = 64
DIFF_HEADS = 8
ROPE_BASE = 10000.0
S5_GROUP_CH = 16
S5_GROUPS = 64
S5_STATE = 64
FFN_DIM = 2816
MOD_CHUNKS = 6
EPS = 1e-6

SUBLANES = 8
LANES = 128
VMEM_LIMIT_BYTES = 56 * 1024 * 1024

ROW_TILE = 1024
FFN_ROW_TILE = 512
FFN_CHUNK = 256
S5_CHUNK = 16
ATT_KEY_CHUNK = 256


def _cparams(*sem):
    return pltpu.CompilerParams(dimension_semantics=sem, vmem_limit_bytes=VMEM_LIMIT_BYTES)


def _silu(x):
    return x * jax.nn.sigmoid(x)


def _rms(x, gain):
    return x * lax.rsqrt(jnp.mean(x * x, axis=-1, keepdims=True) + EPS) * gain


def _mod_slice(mod_ref, idx):
    return mod_ref[0, :, idx * D_MODEL:(idx + 1) * D_MODEL]


def _rms_mod(x, gain, mod_ref, shift_idx, scale_idx):
    return _rms(x, gain) * (1.0 + _mod_slice(mod_ref, scale_idx)) + _mod_slice(mod_ref, shift_idx)


def _residual(x, y, gain, mod_ref, gate_idx):
    return x + _mod_slice(mod_ref, gate_idx) * _rms(y, gain)


def _dot(a, b):
    return jnp.dot(a.astype(BF16), b.astype(BF16), preferred_element_type=F32)


def _dot_nt(a, b):
    return lax.dot_general(a.astype(BF16), b.astype(BF16), (((1,), (1,)), ((), ())),
                           preferred_element_type=F32)


def _dot_tn(a, b):
    return lax.dot_general(a.astype(BF16), b.astype(BF16), (((0,), (0,)), ((), ())),
                           preferred_element_type=F32)


def _dot_f32(a, b):
    return jnp.dot(a, b, preferred_element_type=F32, precision=lax.Precision.HIGHEST)


def _mod_kernel(c_ref, w_ref, b_ref, o_ref):
    o_ref[0] = _dot(_silu(c_ref[...]), w_ref[0]) + b_ref[0]


def _modulation(cond, w_mod, b_mod):
    rows = cond.shape[0]
    n = w_mod.shape[-1]
    tn = 1536
    return pl.pallas_call(
        _mod_kernel,
        grid=(DEPTH, n // tn),
        in_specs=[pl.BlockSpec((rows, D_MODEL), lambda l, j: (0, 0)),
                  pl.BlockSpec((1, D_MODEL, tn), lambda l, j: (l, 0, j)),
                  pl.BlockSpec((1, 1, tn), lambda l, j: (l, 0, j))],
        out_specs=pl.BlockSpec((1, rows, tn), lambda l, j: (l, 0, j)),
        out_shape=jax.ShapeDtypeStruct((DEPTH, rows, n), F32),
        compiler_params=_cparams("parallel", "parallel"),
        name="modulation",
    )(cond, w_mod, b_mod.reshape(DEPTH, 1, n))


def _norm_linear_kernel(x_ref, mod_ref, g_ref, w_ref, o_ref, h_scr):
    @pl.when(pl.program_id(1) == 0)
    def _():
        h_scr[...] = _rms_mod(x_ref[...], g_ref[...], mod_ref, 0, 1).astype(BF16)

    o_ref[...] = jnp.dot(h_scr[...], w_ref[...].astype(BF16), preferred_element_type=F32)


def _mod_spec(tm, rows_per_mod):
    return pl.BlockSpec((1, 1, MOD_CHUNKS * D_MODEL), lambda i, *_: ((i * tm) // rows_per_mod, 0, 0))


def _norm_linear(x, mod, gain, w, rows_per_mod, tn):
    t = x.shape[0]
    n = w.shape[1]
    tm = min(ROW_TILE, rows_per_mod)
    return pl.pallas_call(
        _norm_linear_kernel,
        grid=(t // tm, n // tn),
        in_specs=[pl.BlockSpec((tm, D_MODEL), lambda i, j: (i, 0)),
                  _mod_spec(tm, rows_per_mod),
                  pl.BlockSpec((1, D_MODEL), lambda i, j: (0, 0)),
                  pl.BlockSpec((D_MODEL, tn), lambda i, j: (0, j))],
        out_specs=pl.BlockSpec((tm, tn), lambda i, j: (i, j)),
        out_shape=jax.ShapeDtypeStruct((t, n), F32),
        scratch_shapes=[pltpu.VMEM((tm, D_MODEL), BF16)],
        compiler_params=_cparams("parallel", "arbitrary"),
        name="norm_linear",
    )(x, mod, gain, w)


def _norm_mod_kernel(x_ref, mod_ref, g_ref, o_ref):
    o_ref[...] = _rms_mod(x_ref[...], g_ref[...], mod_ref, 0, 1)


def _norm_mod(x, mod, gain, rows_per_mod):
    t = x.shape[0]
    tm = min(ROW_TILE, rows_per_mod)
    return pl.pallas_call(
        _norm_mod_kernel,
        grid=(t // tm,),
        in_specs=[pl.BlockSpec((tm, D_MODEL), lambda i: (i, 0)),
                  _mod_spec(tm, rows_per_mod),
                  pl.BlockSpec((1, D_MODEL), lambda i: (0, 0))],
        out_specs=pl.BlockSpec((tm, D_MODEL), lambda i: (i, 0)),
        out_shape=jax.ShapeDtypeStruct((t, D_MODEL), F32),
        compiler_params=_cparams("parallel"),
        name="norm_mod",
    )(x, mod, gain)


def _plain_out_kernel(x_ref, y_ref, mod_ref, g_ref, w_ref, o_ref):
    y = _dot(y_ref[...], w_ref[...])
    o_ref[...] = _residual(x_ref[...], y, g_ref[...], mod_ref, 2)


def _gdn_out_kernel(x_ref, of_ref, ob_ref, gate_ref, hn_ref, mod_ref, g_ref, w_ref, o_ref, y_scr):
    o = of_ref[...] + ob_ref[...]
    gate = _silu(gate_ref[...])
    hn = hn_ref[...]
    for h in range(GDN_HEADS):
        sl = slice(h * GDN_DV, (h + 1) * GDN_DV)
        y_scr[:, sl] = (_rms(o[:, sl], hn) * gate[:, sl]).astype(BF16)
    y = jnp.dot(y_scr[...], w_ref[...].astype(BF16), preferred_element_type=F32)
    o_ref[...] = _residual(x_ref[...], y, g_ref[...], mod_ref, 2)


def _s5_out_kernel(x_ref, h_ref, ys_ref, d_ref, mod_ref, g_ref, w_ref, o_ref):
    y = jax.nn.gelu(d_ref[...] * h_ref[...] + ys_ref[...])
    z = _dot(y, w_ref[...])
    val = z[:, :D_MODEL]
    gate = z[:, D_MODEL:]
    o_ref[...] = _residual(x_ref[...], val * jax.nn.sigmoid(gate), g_ref[...], mod_ref, 2)


def _row_spec(tm, width=D_MODEL, col=0):
    return pl.BlockSpec((tm, width), lambda i: (i, col))


def _const_spec(shape):
    return pl.BlockSpec(shape, lambda i: (0,) * len(shape))


def _plain_out(x, y, mod, gain, w, rows_per_mod):
    t = x.shape[0]
    tm = min(ROW_TILE, rows_per_mod)
    return pl.pallas_call(
        _plain_out_kernel,
        grid=(t // tm,),
        in_specs=[_row_spec(tm), _row_spec(tm), _mod_spec(tm, rows_per_mod),
                  _const_spec((1, D_MODEL)), _const_spec((D_MODEL, D_MODEL))],
        out_specs=_row_spec(tm),
        out_shape=jax.ShapeDtypeStruct((t, D_MODEL), F32),
        compiler_params=_cparams("parallel"),
        name="plain_out",
    )(x, y, mod, gain, w)


def _gdn_out(x, o_f, o_b, proj, head_gain, mod, gain, w, rows_per_mod):
    t = x.shape[0]
    tm = min(ROW_TILE, rows_per_mod)
    return pl.pallas_call(
        _gdn_out_kernel,
        grid=(t // tm,),
        in_specs=[_row_spec(tm), _row_spec(tm), _row_spec(tm),
                  _row_spec(tm, D_MODEL, 3),
                  _const_spec((1, GDN_DV)), _mod_spec(tm, rows_per_mod),
                  _const_spec((1, D_MODEL)), _const_spec((D_MODEL, D_MODEL))],
        out_specs=_row_spec(tm),
        out_shape=jax.ShapeDtypeStruct((t, D_MODEL), F32),
        scratch_shapes=[pltpu.VMEM((tm, D_MODEL), BF16)],
        compiler_params=_cparams("parallel"),
        name="gdn_out",
    )(x, o_f, o_b, proj, head_gain, mod, gain, w)


def _s5_out(x, h, y_scan, d_skip, mod, gain, w, rows_per_mod):
    t = x.shape[0]
    tm = min(ROW_TILE // 2, rows_per_mod)
    return pl.pallas_call(
        _s5_out_kernel,
        grid=(t // tm,),
        in_specs=[_row_spec(tm), _row_spec(tm), _row_spec(tm), _const_spec((1, D_MODEL)),
                  _mod_spec(tm, rows_per_mod), _const_spec((1, D_MODEL)),
                  _const_spec((D_MODEL, 2 * D_MODEL))],
        out_specs=_row_spec(tm),
        out_shape=jax.ShapeDtypeStruct((t, D_MODEL), F32),
        compiler_params=_cparams("parallel"),
        name="s5_out",
    )(x, h, y_scan, d_skip, mod, gain, w)


def _ffn_kernel(x_ref, xp_ref, xn_ref, mod_ref, g2_ref, g3_ref, wu_ref, cw_ref, wd_ref, o_ref,
                h_scr, z_scr, act_scr, *, tm, seq_len):
    i = pl.program_id(0)
    ck = FFN_CHUNK
    g2 = g2_ref[...]
    keep_prev = jnp.where((i * tm) % seq_len == 0, 0.0, 1.0)
    keep_next = jnp.where(((i + 1) * tm) % seq_len == 0, 0.0, 1.0)
    h_scr[0:SUBLANES] = (_rms_mod(xp_ref[...], g2, mod_ref, 3, 4) * keep_prev).astype(BF16)
    h_scr[SUBLANES:SUBLANES + tm] = _rms_mod(x_ref[...], g2, mod_ref, 3, 4).astype(BF16)
    h_scr[SUBLANES + tm:] = (_rms_mod(xn_ref[...], g2, mod_ref, 3, 4) * keep_next).astype(BF16)
    h = h_scr[...]

    def conv(z_ref, cols, col0):
        cw = cw_ref[:, col0:col0 + ck]
        return (cw[0:1] * z_ref[SUBLANES - 1:SUBLANES - 1 + tm, cols]
                + cw[1:2] * z_ref[SUBLANES:SUBLANES + tm, cols]
                + cw[2:3] * z_ref[SUBLANES + 1:SUBLANES + 1 + tm, cols])

    for k in range(FFN_DIM // ck):
        z_ref = z_scr.at[k % 2]
        g0 = k * ck
        v0 = FFN_DIM + k * ck
        z_ref[:, :ck] = jnp.dot(h, wu_ref[:, g0:g0 + ck].astype(BF16), preferred_element_type=F32)
        z_ref[:, ck:] = jnp.dot(h, wu_ref[:, v0:v0 + ck].astype(BF16), preferred_element_type=F32)
        gate = conv(z_ref, slice(0, ck), g0)
        val = conv(z_ref, slice(ck, 2 * ck), v0)
        act_scr[:, g0:g0 + ck] = (_silu(gate) * val).astype(BF16)
    y = jnp.dot(act_scr[...], wd_ref[...].astype(BF16), preferred_element_type=F32)
    o_ref[...] = _residual(x_ref[...], y, g3_ref[...], mod_ref, 5)


def _conv_ffn(x, mod, gain2, gain3, w_up, conv_w, w_down, seq_len, rows_per_mod):
    t = x.shape[0]
    tm = min(FFN_ROW_TILE, seq_len)
    nblk8 = t // SUBLANES
    r8 = tm // SUBLANES
    resident = lambda shape: pl.BlockSpec(shape, lambda i: (0, 0), pipeline_mode=pl.Buffered(1))
    return pl.pallas_call(
        functools.partial(_ffn_kernel, tm=tm, seq_len=seq_len),
        grid=(t // tm,),
        in_specs=[
            pl.BlockSpec((tm, D_MODEL), lambda i: (i, 0)),
            pl.BlockSpec((SUBLANES, D_MODEL), lambda i: (jnp.maximum(i * r8 - 1, 0), 0)),
            pl.BlockSpec((SUBLANES, D_MODEL), lambda i: (jnp.minimum((i + 1) * r8, nblk8 - 1), 0)),
            _mod_spec(tm, rows_per_mod),
            _const_spec((1, D_MODEL)),
            _const_spec((1, D_MODEL)),
            resident((D_MODEL, 2 * FFN_DIM)),
            resident((3, 2 * FFN_DIM)),
            resident((FFN_DIM, D_MODEL)),
        ],
        out_specs=pl.BlockSpec((tm, D_MODEL), lambda i: (i, 0)),
        out_shape=jax.ShapeDtypeStruct((t, D_MODEL), F32),
        scratch_shapes=[pltpu.VMEM((tm + 2 * SUBLANES, D_MODEL), BF16),
                        pltpu.VMEM((2, tm + 2 * SUBLANES, 2 * FFN_CHUNK), F32),
                        pltpu.VMEM((tm, FFN_DIM), BF16)],
        compiler_params=_cparams("parallel"),
        name="conv_ffn",
    )(x, x, x, mod, gain2, gain3, w_up, conv_w, w_down)


ROPE_HALF = DIFF_DH // 4
LOG2E = 1.4426950408889634


def _rope(x, cos, sin_signed):
    lane = lax.broadcasted_iota(jnp.int32, x.shape, 1)
    first = (lane % (2 * ROPE_HALF)) < ROPE_HALF
    rot = jnp.where(first, pltpu.roll(x, LANES - ROPE_HALF, 1), pltpu.roll(x, ROPE_HALF, 1))
    return x * cos + rot * sin_signed


def _diff_attn_body(q_ref, k_ref, v_ref, ck_ref, cv_ref, cos_ref, sin_ref, lam_ref, subln_ref,
                    o_ref, k_scr, vt_scr, s_scr, *, n_cache, tq, lam_init):
    qi = pl.program_id(2)
    seq_len = k_ref.shape[0]
    n_keys = n_cache + seq_len
    rope = cos_ref is not None
    tr = 512 if seq_len % 512 == 0 else seq_len

    @pl.when(qi == 0)
    def _():
        k = k_ref[...]
        if rope:
            k = _rope(k, cos_ref[...], sin_ref[...])
        if n_cache:
            k_scr[0:n_cache] = ck_ref[0].astype(BF16)
            vt_scr[:, 0:n_cache] = cv_ref[0].T.astype(BF16)
        k_scr[n_cache:] = k.astype(BF16)
        for r in range(0, seq_len, tr):
            vt_scr[:, n_cache + r:n_cache + r + tr] = v_ref[r:r + tr, :].T.astype(BF16)

    q = q_ref[...]
    if rope:
        r0 = pl.multiple_of(qi * tq, tq)
        q = _rope(q, cos_ref[pl.ds(r0, tq), :], sin_ref[pl.ds(r0, tq), :])
    q = q * (DIFF_DH ** -0.5 * LOG2E)
    lane = lax.broadcasted_iota(jnp.int32, q.shape, 1)
    qz = (jnp.where(lane < DIFF_DH, q, 0.0).astype(BF16), jnp.where(lane >= DIFF_DH, q, 0.0).astype(BF16))

    kc = ATT_KEY_CHUNK
    n_chunks = n_keys // kc
    part = lambda x, op: functools.reduce(op, [x[r:r + SUBLANES] for r in range(0, kc, SUBLANES)])
    m = [None, None]
    for j in range(n_chunks):
        kch = k_scr[j * kc:(j + 1) * kc, :]
        for c in range(2):
            s = lax.dot_general(kch, qz[c], (((1,), (1,)), ((), ())), preferred_element_type=F32)
            s_scr[c, j * kc:(j + 1) * kc, :] = s
            pm = part(s, jnp.maximum)
            m[c] = pm if m[c] is None else jnp.maximum(m[c], pm)
    m = [jnp.max(x, axis=0, keepdims=True) for x in m]
    l = [None, None]
    acc = [None, None]
    for j in range(n_chunks):
        vt = vt_scr[:, j * kc:(j + 1) * kc]
        for c in range(2):
            p = jnp.exp2(s_scr[c, j * kc:(j + 1) * kc, :] - m[c])
            ps = part(p, jnp.add)
            pv = jnp.dot(vt, p.astype(BF16), preferred_element_type=F32)
            l[c] = ps if l[c] is None else l[c] + ps
            acc[c] = pv if acc[c] is None else acc[c] + pv
    l = [jnp.sum(x, axis=0, keepdims=True) for x in l]

    lf = lam_ref[...]
    lam = (jnp.exp(jnp.sum(lf[0:1] * lf[1:2], axis=1, keepdims=True))
           - jnp.exp(jnp.sum(lf[2:3] * lf[3:4], axis=1, keepdims=True)) + lam_init)
    o = acc[0] * (1.0 / l[0]) - lam * (acc[1] * (1.0 / l[1]))
    ms = jnp.mean(o * o, axis=0, keepdims=True)
    y = (o * lax.rsqrt(ms + EPS) * subln_ref[...]) * (1.0 - lam_init)
    o_ref[...] = y.T


def _diff_attn_cached_kernel(q_ref, k_ref, v_ref, ck_ref, cv_ref, cos_ref, sin_ref, lam_ref, subln_ref,
                             o_ref, k_scr, vt_scr, s_scr, **kw):
    _diff_attn_body(q_ref, k_ref, v_ref, ck_ref, cv_ref, cos_ref, sin_ref, lam_ref, subln_ref,
                    o_ref, k_scr, vt_scr, s_scr, **kw)


def _diff_attn_context_kernel(q_ref, k_ref, v_ref, lam_ref, subln_ref, o_ref, k_scr, vt_scr, s_scr, **kw):
    _diff_attn_body(q_ref, k_ref, v_ref, None, None, None, None, lam_ref, subln_ref,
                    o_ref, k_scr, vt_scr, s_scr, **kw)


def _diff_attention(qkv, lam_vecs, subln, lam_init, nb, seq_len, cache=None, rope_tables=None):
    tq = 256
    nq = seq_len // tq
    width = 2 * DIFF_DH
    n_cache = 0 if cache is None else cache[0].shape[1]
    n_keys = n_cache + seq_len
    in_specs = [pl.BlockSpec((tq, width), lambda b, h, i: (b * nq + i, h)),
                pl.BlockSpec((seq_len, width), lambda b, h, i: (b, DIFF_HEADS + h)),
                pl.BlockSpec((seq_len, width), lambda b, h, i: (b, 2 * DIFF_HEADS + h))]
    args = [qkv, qkv, qkv]
    if cache is not None:
        in_specs += [pl.BlockSpec((1, n_cache, width), lambda b, h, i: (b, 0, h))] * 2
        in_specs += [pl.BlockSpec((seq_len, width), lambda b, h, i: (0, 0))] * 2
        args += [cache[0], cache[1], rope_tables[0], rope_tables[1]]
        kern = _diff_attn_cached_kernel
    else:
        kern = _diff_attn_context_kernel
    in_specs += [pl.BlockSpec((4, DIFF_DH), lambda b, h, i: (0, 0)),
                 pl.BlockSpec((width, 1), lambda b, h, i: (0, 0))]
    args += [lam_vecs, subln.reshape(width, 1)]
    return pl.pallas_call(
        functools.partial(kern, n_cache=n_cache, tq=tq, lam_init=lam_init),
        grid=(nb, DIFF_HEADS, nq),
        in_specs=in_specs,
        out_specs=pl.BlockSpec((tq, width), lambda b, h, i: (b * nq + i, h)),
        out_shape=jax.ShapeDtypeStruct((nb * seq_len, D_MODEL), F32),
        scratch_shapes=[pltpu.VMEM((n_keys, width), BF16), pltpu.VMEM((width, n_keys), BF16),
                        pltpu.VMEM((2, n_keys, tq), F32)],
        compiler_params=_cparams("parallel", "parallel", "arbitrary"),
        name="diff_attention",
    )(*args)


def _rope_tables(n_tokens):
    rows = (jnp.arange(n_tokens, dtype=jnp.int32) // GRID_W).astype(F32)
    cols = (jnp.arange(n_tokens, dtype=jnp.int32) % GRID_W).astype(F32)
    n_freq = DIFF_DH // 4
    inv_freq = ROPE_BASE ** (-jnp.arange(n_freq, dtype=F32) / n_freq)
    ang_r = rows[:, None] * inv_freq
    ang_c = cols[:, None] * inv_freq
    ang = jnp.concatenate([ang_r, ang_r, ang_c, ang_c] * 2, axis=-1)
    sign = jnp.where((jnp.arange(2 * DIFF_DH) % (2 * ROPE_HALF)) < ROPE_HALF, -1.0, 1.0).astype(F32)
    return jnp.cos(ang), jnp.sin(ang) * sign


GDN_QKV = 3 * GDN_HEADS * GDN_DK
GDN_PROJ = GDN_QKV + D_MODEL + LANES
GDN_EXT = GDN_CHUNK + 2 * SUBLANES


def _cumsum_rows(x, reverse):
    n = x.shape[0]
    row = lax.broadcasted_iota(jnp.int32, x.shape, 0)
    k = 1
    while k < n:
        if reverse:
            x = x + jnp.where(row < n - k, pltpu.roll(x, n - k, 0), 0.0)
        else:
            x = x + jnp.where(row >= k, pltpu.roll(x, k, 0), 0.0)
        k *= 2
    return x


def _gdn_units(units, eye, blocks):
    c = GDN_CHUNK
    for n in units:
        n["decay"] = jnp.where(n["incl"], jnp.exp(jnp.where(n["incl"], n["gcol"] - n["grow"], 0.0)), 0.0)
        n["kb"] = n["k"] * n["bcol"]
        n["kt"] = n["k"].T
    for n in units:
        a = _dot(jnp.concatenate([n["kb"], n["q"]], axis=0), n["kt"])
        n["lower"] = jnp.where(n["strict"], a[:c] * n["decay"], 0.0)
        n["intra"] = a[c:] * n["decay"]
        n["t_off"] = -jnp.where(blocks[0], n["lower"], 0.0)
    for blk in blocks[1:]:
        for n in units:
            n["p"] = _dot(eye + n["t_off"], jnp.where(blk, n["lower"], 0.0))
        for n in units:
            n["t_off"] = n["t_off"] - _dot(n["p"], eye + n["t_off"])
    for n in units:
        egc = jnp.exp(n["gcol"])
        rhs = jnp.concatenate([n["v"] * n["bcol"], n["kb"] * egc], axis=1)
        sol = rhs + _dot(n["t_off"], rhs)
        n["u"] = sol[:, :GDN_DV]
        n["wq"] = jnp.concatenate([sol[:, GDN_DV:], n["q"] * egc], axis=0)
    for n in units:
        ws = _dot(n["wq"], n["s"])
        n["v_new"] = n["u"] - ws[:c]
        n["qs"] = ws[c:]
        n["ikt"] = jnp.concatenate([n["intra"], n["kt"] * jnp.exp(n["gtot"] - n["grow"])], axis=0)
    outs = []
    for n in units:
        r = _dot(n["ikt"], n["v_new"])
        outs.append((n["qs"] + r[:c], n["s"] * jnp.exp(n["gtot"]) + r[c:]))
    return outs


def _gdn_kernel(fm_ref, fp_ref, fn_ref, fab_ref, bm_ref, bp_ref, bn_ref, bab_ref, cw_ref, alog_ref,
                dtb_ref, *rest, has_s0):
    if has_s0:
        s0_ref, of_ref, ob_ref, s_ref, ext_scr = rest
    else:
        of_ref, ob_ref, s_ref, ext_scr = rest
    c = pl.program_id(1)
    n = pl.num_programs(1)

    @pl.when(c == 0)
    def _():
        if has_s0:
            s_ref[...] = s0_ref[...]
        else:
            s_ref[...] = jnp.zeros_like(s_ref)

    ri = lax.broadcasted_iota(jnp.int32, (GDN_CHUNK, GDN_CHUNK), 0)
    ci = lax.broadcasted_iota(jnp.int32, (GDN_CHUNK, GDN_CHUNK), 1)
    eye = jnp.where(ri == ci, 1.0, 0.0)
    blocks = []
    size = 1
    while size < GDN_CHUNK:
        blocks.append((ri // (2 * size) == ci // (2 * size)) & (ri // size != ci // size))
        size *= 2
    cw = cw_ref[...]
    neg_a = -jnp.exp(alog_ref[...])
    dtb = dtb_ref[...]

    units = []
    for d, (m_ref, p_ref, n_ref, ab_ref, o_ref) in enumerate(
            ((fm_ref, fp_ref, fn_ref, fab_ref, of_ref), (bm_ref, bp_ref, bn_ref, bab_ref, ob_ref))):
        cc = c if d == 0 else n - 1 - c
        ext_scr[0:SUBLANES] = p_ref[...] * jnp.where(cc == 0, 0.0, 1.0)
        ext_scr[SUBLANES:SUBLANES + GDN_CHUNK] = m_ref[...]
        ext_scr[SUBLANES + GDN_CHUNK:] = n_ref[...] * jnp.where(cc == n - 1, 0.0, 1.0)
        conv = (cw[0:1] * ext_scr[SUBLANES - 1:SUBLANES - 1 + GDN_CHUNK]
                + cw[1:2] * m_ref[...]
                + cw[2:3] * ext_scr[SUBLANES + 1:SUBLANES + 1 + GDN_CHUNK])
        qkv = _silu(conv)

        ab = ab_ref[...]
        x = ab + dtb
        g = neg_a * (jnp.maximum(x, 0.0) + jnp.log(1.0 + jnp.exp(-jnp.abs(x))))
        beta = jax.nn.sigmoid(ab)
        gc = _cumsum_rows(g, reverse=(d == 1))
        gct = jnp.concatenate([gc, gc], axis=0).T
        if d == 0:
            incl, strict, last = ri >= ci, ri > ci, GDN_CHUNK - 1
        else:
            incl, strict, last = ri <= ci, ri < ci, 0

        for h in range(GDN_HEADS):
            col = d * GDN_HEADS + h
            q = qkv[:, h * GDN_DK:(h + 1) * GDN_DK]
            k = qkv[:, (GDN_HEADS + h) * GDN_DK:(GDN_HEADS + h + 1) * GDN_DK]
            v = qkv[:, (2 * GDN_HEADS + h) * GDN_DK:(2 * GDN_HEADS + h + 1) * GDN_DK]
            q = q * lax.rsqrt(jnp.sum(q * q, axis=-1, keepdims=True) + EPS) * (GDN_DK ** -0.5)
            k = k * lax.rsqrt(jnp.sum(k * k, axis=-1, keepdims=True) + EPS)
            gcol = gc[:, col:col + 1]
            grow = gct[col:col + 1, 0:GDN_CHUNK]
            bcol = beta[:, 2 * GDN_HEADS + col:2 * GDN_HEADS + col + 1]
            gtot = gc[last:last + 1, col:col + 1]
            units.append(dict(q=q, k=k, v=v, gcol=gcol, grow=grow, bcol=bcol, gtot=gtot, s=s_ref[0, d, h],
                              incl=incl, strict=strict))

    outs = _gdn_units(units, eye, blocks)
    for d, o_ref in enumerate((of_ref, ob_ref)):
        for h in range(GDN_HEADS):
            o, s_new = outs[d * GDN_HEADS + h]
            o_ref[:, h * GDN_DV:(h + 1) * GDN_DV] = o
            s_ref[0, d, h] = s_new


def _gdn_scan(proj, conv_w, a_log, dt_bias, nb, seq_len, s0=None):
    n = seq_len // GDN_CHUNK
    t = nb * seq_len
    r8 = GDN_CHUNK // SUBLANES
    nblk8 = t // SUBLANES
    ab_col = (GDN_QKV + D_MODEL) // LANES
    pad = lambda p: jnp.pad(p.reshape(1, 2 * GDN_HEADS), ((0, 0), (0, LANES - 2 * GDN_HEADS)))

    def chunk_specs(chunk_of):
        return [pl.BlockSpec((GDN_CHUNK, GDN_QKV), lambda b, c: (b * n + chunk_of(c), 0)),
                pl.BlockSpec((SUBLANES, GDN_QKV), lambda b, c: (jnp.maximum((b * n + chunk_of(c)) * r8 - 1, 0), 0)),
                pl.BlockSpec((SUBLANES, GDN_QKV),
                             lambda b, c: (jnp.minimum((b * n + chunk_of(c) + 1) * r8, nblk8 - 1), 0)),
                pl.BlockSpec((GDN_CHUNK, LANES), lambda b, c: (b * n + chunk_of(c), ab_col))]

    state_spec = pl.BlockSpec((1, 2, GDN_HEADS, GDN_DK, GDN_DV), lambda b, c: (b, 0, 0, 0, 0))
    in_specs = (chunk_specs(lambda c: c) + chunk_specs(lambda c: n - 1 - c)
                + [pl.BlockSpec((3, GDN_QKV), lambda b, c: (0, 0)),
                   pl.BlockSpec((1, LANES), lambda b, c: (0, 0)),
                   pl.BlockSpec((1, LANES), lambda b, c: (0, 0))])
    args = [proj] * 8 + [conv_w, pad(a_log), pad(dt_bias)]
    if s0 is not None:
        in_specs.append(state_spec)
        args.append(s0)
    hv = GDN_HEADS * GDN_DV
    return pl.pallas_call(
        functools.partial(_gdn_kernel, has_s0=s0 is not None),
        grid=(nb, n),
        in_specs=in_specs,
        out_specs=[pl.BlockSpec((GDN_CHUNK, hv), lambda b, c: (b * n + c, 0)),
                   pl.BlockSpec((GDN_CHUNK, hv), lambda b, c: (b * n + n - 1 - c, 0)),
                   state_spec],
        out_shape=[jax.ShapeDtypeStruct((t, hv), F32), jax.ShapeDtypeStruct((t, hv), F32),
                   jax.ShapeDtypeStruct((nb, 2, GDN_HEADS, GDN_DK, GDN_DV), F32)],
        scratch_shapes=[pltpu.VMEM((GDN_EXT, GDN_QKV), F32)],
        compiler_params=_cparams("parallel", "arbitrary"),
        name="gdn_scan",
    )(*args)


def _gdn_mixer(x, mod, gains, w_cat, conv_w, a_log, dt_bias, head_gain, w_out, nb, seq_len, rows_per_mod,
               s0=None):
    proj = _norm_linear(x, mod, gains[0][None], w_cat, rows_per_mod, LANES * 11)
    o_f, o_b, s_fin = _gdn_scan(proj, conv_w, a_log, dt_bias, nb, seq_len, s0)
    x = _gdn_out(x, o_f, o_b, proj, head_gain[None], mod, gains[1][None], w_out, rows_per_mod)
    return x, s_fin


def _diff_mixer(x, mod, gains, w_qkv, lam_vecs, subln, w_out, lam_init, nb, seq_len, rows_per_mod,
                cache=None, rope_tables=None):
    qkv = _norm_linear(x, mod, gains[0][None], w_qkv, rows_per_mod, D_MODEL)
    heads = _diff_attention(qkv, lam_vecs, subln, lam_init, nb, seq_len, cache, rope_tables)
    x = _plain_out(x, heads, mod, gains[1][None], w_out, rows_per_mod)
    return x, qkv


S5_IN = S5_CHUNK * S5_GROUP_CH
S5_BITS = 5


def _cmul(ar, ai, br, bi):
    return ar * br - ai * bi, ar * bi + ai * br


def _discretize(a_re, a_im, log_dt):
    dt = jnp.exp(log_dt)
    mag = jnp.exp(dt * a_re)
    abar_re = mag * jnp.cos(dt * a_im)
    abar_im = mag * jnp.sin(dt * a_im)
    den = a_re * a_re + a_im * a_im
    f_re = ((abar_re - 1.0) * a_re + abar_im * a_im) / den
    f_im = (abar_im * a_re - (abar_re - 1.0) * a_im) / den
    return abar_re, abar_im, f_re, f_im


def _cpow(ar, ai, expo):
    pr = jnp.ones(expo.shape, F32)
    pi = jnp.zeros(expo.shape, F32)
    for b in range(S5_BITS):
        bit = ((expo >> b) & 1) == 1
        pr, pi = _cmul(pr, pi, jnp.where(bit, ar, 1.0), jnp.where(bit, ai, 0.0))
        if b + 1 < S5_BITS:
            ar, ai = _cmul(ar, ai, ar, ai)
    return pr, pi


def _s5_ops_kernel(arow_ref, acol_ref, bpc_ref, bt_ref, ccat_ref, csep_ref, mc_ref, ef_ref, a16_ref):
    arow = arow_ref[0]
    ar, ai, fr, fi = _discretize(arow[0:1], arow[1:2], arow[2:3])
    row = lax.broadcasted_iota(jnp.int32, (S5_IN, LANES), 0)
    lane = lax.broadcasted_iota(jnp.int32, (S5_IN, LANES), 1)
    tok = row // S5_GROUP_CH
    fwd = lane < S5_STATE
    tile = lambda x: jnp.concatenate([x] * S5_CHUNK, axis=0)

    pr, pi = _cpow(ar, ai, jnp.where(fwd, S5_CHUNK - 1 - tok, tok))
    bt = bt_ref[0]
    bbr, bbi = _cmul(fr, fi, bt[0], bt[1])
    er, ei = _cmul(pr, pi, tile(bbr), tile(bbi))
    ef_ref[0, 0] = er
    ef_ref[0, 1] = ei
    pr, pi = _cpow(ar, ai, jnp.where(fwd, tok + 1, S5_CHUNK - tok))
    cc = ccat_ref[0]
    gr, gi = _cmul(pr, pi, tile(cc[0]), tile(cc[1]))
    ef_ref[0, 2] = gr
    ef_ref[0, 3] = -gi
    p16r, p16i = _cpow(ar, ai, jnp.full((1, LANES), S5_CHUNK, jnp.int32))
    a16_ref[0, 0:1] = p16r
    a16_ref[0, 1:2] = p16i

    acol = acol_ref[0]
    lane_w = lax.broadcasted_iota(jnp.int32, (S5_GROUP_CH, S5_IN), 1)
    place = jnp.where(lane_w % S5_GROUP_CH == lax.broadcasted_iota(jnp.int32, (S5_GROUP_CH, S5_IN), 0), 1.0, 0.0)
    lag = lax.broadcasted_iota(jnp.int32, (S5_STATE, S5_IN), 1) // S5_GROUP_CH
    kw = []
    for d in range(2):
        cr, ci, gr_, gi_ = _discretize(acol[:, 3 * d:3 * d + 1], acol[:, 3 * d + 1:3 * d + 2],
                                       acol[:, 3 * d + 2:3 * d + 3])
        bbr, bbi = _cmul(gr_, gi_, bpc_ref[0, 2 * d], bpc_ref[0, 2 * d + 1])
        pr, pi = _cpow(jnp.broadcast_to(cr, lag.shape), jnp.broadcast_to(ci, lag.shape),
                       S5_CHUNK - 1 - lag if d == 0 else lag)
        hr, hi = _cmul(pr, pi, _dot_f32(bbr, place), _dot_f32(bbi, place))
        kw.append(_dot_f32(csep_ref[0, 2 * d], hr) - _dot_f32(csep_ref[0, 2 * d + 1], hi))
    for t in range(S5_CHUNK):
        sh = (S5_CHUNK - 1 - t) * S5_GROUP_CH
        m_f = jnp.where(lane_w < (t + 1) * S5_GROUP_CH, pltpu.roll(kw[0], (S5_IN - sh) % S5_IN, 1), 0.0)
        m_b = jnp.where(lane_w >= t * S5_GROUP_CH, pltpu.roll(kw[1], t * S5_GROUP_CH, 1), 0.0)
        mc_ref[0, t * S5_GROUP_CH:(t + 1) * S5_GROUP_CH, :] = m_f + m_b


def _s5_operators(a_re, a_im, log_dt, b_re, b_im, c_re, c_im):
    g = S5_GROUPS
    cat = lambda x: jnp.concatenate([x[0], x[1]], axis=-1)
    dt_row = jnp.broadcast_to(log_dt[:, :, None], (2, g, S5_STATE))
    arow = jnp.stack([cat(a_re), cat(a_im), cat(dt_row)], axis=1)
    acol = jnp.stack([a_re[0], a_im[0], dt_row[0], a_re[1], a_im[1], dt_row[1]], axis=-1)
    bpc = jnp.stack([b_re[0], b_im[0], b_re[1], b_im[1]], axis=1)
    swap = lambda x: jnp.swapaxes(x, -1, -2)
    bt = jnp.stack([cat(swap(b_re)), cat(swap(b_im))], axis=1)
    ccat = jnp.stack([cat(c_re), cat(c_im)], axis=1)
    csep = jnp.stack([c_re[0], c_im[0], c_re[1], c_im[1]], axis=1)
    spec = lambda *s: pl.BlockSpec((1,) + s, lambda i: (i,) + (0,) * len(s))
    return pl.pallas_call(
        _s5_ops_kernel,
        grid=(g,),
        in_specs=[spec(3, LANES), spec(S5_STATE, 6), spec(4, S5_STATE, S5_GROUP_CH),
                  spec(2, S5_GROUP_CH, LANES), spec(2, S5_GROUP_CH, LANES), spec(4, S5_GROUP_CH, S5_STATE)],
        out_specs=[spec(S5_IN, S5_IN), spec(4, S5_IN, LANES), spec(2, LANES)],
        out_shape=[jax.ShapeDtypeStruct((g, S5_IN, S5_IN), F32),
                   jax.ShapeDtypeStruct((g, 4, S5_IN, LANES), F32),
                   jax.ShapeDtypeStruct((g, 2, LANES), F32)],
        compiler_params=_cparams("parallel"),
        name="s5_operators",
    )(arow, acol, bpc, bt, ccat, csep)


def _s5_scan_kernel(u_ref, mc_ref, ef_ref, a16_ref, *rest, nb, has_x0):
    if has_x0:
        x0r_ref, x0i_ref, y_ref, fr_ref, fi_ref, xr_scr, xi_scr = rest
    else:
        y_ref, fr_ref, fi_ref, xr_scr, xi_scr = rest
    u = u_ref[0].astype(BF16)
    rows = u.shape[0]
    nc = rows // nb
    vr = jnp.dot(u, ef_ref[0, 0].astype(BF16), preferred_element_type=F32)
    vi = jnp.dot(u, ef_ref[0, 1].astype(BF16), preferred_element_type=F32)
    row_c = lax.broadcasted_iota(jnp.int32, (rows, LANES), 0) % nc
    fwd = lax.broadcasted_iota(jnp.int32, (rows, LANES), 1) < S5_STATE
    pr = a16_ref[0, 0:1]
    pi = a16_ref[0, 1:2]
    if has_x0:
        expand = lambda x: jnp.broadcast_to(x[:, None, :], (nb, nc, LANES)).reshape(rows, LANES)
        x0r, x0i = expand(x0r_ref[0]), expand(x0i_ref[0])
        first = row_c == jnp.where(fwd, 0, nc - 1)
        ir, ii = _cmul(pr, pi, x0r, x0i)
        vr = vr + jnp.where(first, ir, 0.0)
        vi = vi + jnp.where(first, ii, 0.0)

    def shifted(x, k):
        down = jnp.where(row_c >= k, pltpu.roll(x, k, 0), 0.0)
        up = jnp.where(row_c < nc - k, pltpu.roll(x, rows - k, 0), 0.0)
        return jnp.where(fwd, down, up)

    k = 1
    while k < nc:
        sr, si = _cmul(pr, pi, shifted(vr, k), shifted(vi, k))
        vr, vi = vr + sr, vi + si
        pr, pi = _cmul(pr, pi, pr, pi)
        k *= 2
    er, ei = shifted(vr, 1), shifted(vi, 1)
    if has_x0:
        er = jnp.where(first, x0r, er)
        ei = jnp.where(first, x0i, ei)
    y = lax.dot_general(u, mc_ref[0].astype(BF16), (((1,), (1,)), ((), ())), preferred_element_type=F32)
    y = y + _dot_nt(er, ef_ref[0, 2]) + _dot_nt(ei, ef_ref[0, 3])
    y_ref[0] = y
    xr_scr[...] = vr
    xi_scr[...] = vi
    lane_b = lax.broadcasted_iota(jnp.int32, (nb, LANES), 1) < S5_STATE
    fr_ref[0] = jnp.where(lane_b, xr_scr[pl.ds(nc - 1, nb, stride=nc), :], xr_scr[pl.ds(0, nb, stride=nc), :])
    fi_ref[0] = jnp.where(lane_b, xi_scr[pl.ds(nc - 1, nb, stride=nc), :], xi_scr[pl.ds(0, nb, stride=nc), :])


def _s5_scan(u, ops, nb, x0=None):
    mc, ef, a16 = ops
    g, rows, _ = u.shape
    spec = lambda *s: pl.BlockSpec((1,) + s, lambda i: (i,) + (0,) * len(s))
    in_specs = [spec(rows, S5_IN), spec(S5_IN, S5_IN), spec(4, S5_IN, LANES), spec(2, LANES)]
    args = [u, mc, ef, a16]
    if x0 is not None:
        in_specs += [spec(nb, LANES), spec(nb, LANES)]
        args += list(x0)
    return pl.pallas_call(
        functools.partial(_s5_scan_kernel, nb=nb, has_x0=x0 is not None),
        grid=(g,),
        in_specs=in_specs,
        out_specs=[spec(rows, S5_IN), spec(nb, LANES), spec(nb, LANES)],
        out_shape=[jax.ShapeDtypeStruct((g, rows, S5_IN), F32),
                   jax.ShapeDtypeStruct((g, nb, LANES), F32),
                   jax.ShapeDtypeStruct((g, nb, LANES), F32)],
        scratch_shapes=[pltpu.VMEM((rows, LANES), F32), pltpu.VMEM((rows, LANES), F32)],
        compiler_params=_cparams("parallel"),
        name="s5_scan",
    )(*args)


def _to_chunk_major(h, nb, seq_len):
    nc = seq_len // S5_CHUNK
    h = h.reshape(nb * nc, S5_CHUNK, S5_GROUPS, S5_GROUP_CH)
    return jnp.transpose(h, (2, 0, 1, 3)).reshape(S5_GROUPS, nb * nc, S5_IN)


def _from_chunk_major(y, nb, seq_len):
    nc = seq_len // S5_CHUNK
    y = y.reshape(S5_GROUPS, nb * nc, S5_CHUNK, S5_GROUP_CH)
    return jnp.transpose(y, (1, 2, 0, 3)).reshape(nb * seq_len, D_MODEL)


def _s5_mixer(x, mod, gains, ops, d_skip, w_glu, nb, seq_len, rows_per_mod, cache=None):
    h = _norm_mod(x, mod, gains[0][None], rows_per_mod)
    x0 = None
    if cache is not None:
        x0 = [jnp.transpose(s, (2, 0, 1, 3)).reshape(S5_GROUPS, nb, 2 * S5_STATE) for s in cache]
    y, fin_re, fin_im = _s5_scan(_to_chunk_major(h, nb, seq_len), ops, nb, x0)
    x = _s5_out(x, h, _from_chunk_major(y, nb, seq_len), d_skip[None], mod, gains[1][None], w_glu, rows_per_mod)
    unpack = lambda f: jnp.transpose(f.reshape(S5_GROUPS, nb, 2, S5_STATE), (1, 2, 0, 3))
    return x, (unpack(fin_re), unpack(fin_im))


def kernel(x_prompt, x_sample, state_l0_gdn, cache_l1_k, cache_l1_v, state_l2_s5_re, state_l2_s5_im, state_l3_gdn, c, c_ctx, w_mod, b_mod, norm_gain, w_ffn_up, ffn_conv, w_ffn_down, w_gdn_qkv, gdn_conv, w_gdn_gate, w_gdn_alpha, w_gdn_beta, gdn_a_log, gdn_dt_bias, gdn_norm, w_gdn_out, w_diff_qkv, diff_lam, diff_subln, w_diff_out, s5_a_re, s5_a_im, s5_log_dt, s5_b_re, s5_b_im, s5_c_re, s5_c_im, s5_d, w_s5_glu):
    nbp, lp = x_prompt.shape[:2]
    nbs, ls = x_sample.shape[:2]
    tp, ts = nbp * lp, nbs * ls
    pad_rows = -(nbs + 1) % SUBLANES
    cond = jnp.concatenate([c, c_ctx[None], jnp.zeros((pad_rows, D_MODEL), F32)], axis=0)
    mod_all = _modulation(cond, w_mod, b_mod)
    s5_ops = _s5_operators(s5_a_re[0], s5_a_im[0], s5_log_dt[0], s5_b_re[0], s5_b_im[0], s5_c_re[0], s5_c_im[0])
    rope = _rope_tables(ls)
    past = cache_l1_k.shape[1]
    attn_cache = (cache_l1_k.reshape(nbs, past, D_MODEL), cache_l1_v.reshape(nbs, past, D_MODEL))
    gdn_caches = (state_l0_gdn, state_l3_gdn)
    bf = lambda w: w.astype(BF16)

    xp = x_prompt.reshape(tp, D_MODEL)
    xs = x_sample.reshape(ts, D_MODEL)
    ctx_states = []
    for l in range(DEPTH):
        kind, j = l % 3, l // 3
        gains = norm_gain[l]
        mod_s = mod_all[l, :nbs][:, None, :]
        mod_p = mod_all[l, nbs:nbs + 1][:, None, :]
        if kind == 0:
            w_cat = jnp.concatenate([w_gdn_qkv[j], w_gdn_gate[j], w_gdn_alpha[j], w_gdn_beta[j],
                                     jnp.zeros((D_MODEL, LANES - 4 * GDN_HEADS), F32)], axis=1).astype(BF16)
            gdn = functools.partial(_gdn_mixer, gains=gains, w_cat=w_cat, conv_w=gdn_conv[j], a_log=gdn_a_log[j],
                                    dt_bias=gdn_dt_bias[j], head_gain=gdn_norm[j], w_out=bf(w_gdn_out[j]))
            xp, st = gdn(xp, mod_p, nb=nbp, seq_len=lp, rows_per_mod=tp)
            xs, _ = gdn(xs, mod_s, nb=nbs, seq_len=ls, rows_per_mod=ls, s0=gdn_caches[j])
            ctx_states.append(st)
        elif kind == 1:
            lam_init = 0.8 - 0.6 * math.exp(-0.3 * l)
            attn = functools.partial(_diff_mixer, gains=gains, w_qkv=bf(w_diff_qkv[j]), lam_vecs=diff_lam[j],
                                     subln=diff_subln[j], w_out=bf(w_diff_out[j]), lam_init=lam_init)
            xp, qkv = attn(xp, mod_p, nb=nbp, seq_len=lp, rows_per_mod=tp)
            xs, _ = attn(xs, mod_s, nb=nbs, seq_len=ls, rows_per_mod=ls, cache=attn_cache, rope_tables=rope)
            ctx_states.append((qkv[:, D_MODEL:2 * D_MODEL].reshape(nbp, lp, DIFF_HEADS, 2, DIFF_DH),
                               qkv[:, 2 * D_MODEL:].reshape(nbp, lp, DIFF_HEADS, 2 * DIFF_DH)))
        else:
            s5 = functools.partial(_s5_mixer, gains=gains, ops=s5_ops, d_skip=s5_d[j], w_glu=bf(w_s5_glu[j]))
            xp, st = s5(xp, mod_p, nb=nbp, seq_len=lp, rows_per_mod=tp)
            xs, _ = s5(xs, mod_s, nb=nbs, seq_len=ls, rows_per_mod=ls, cache=(state_l2_s5_re, state_l2_s5_im))
            ctx_states.append(st)
        ffn = functools.partial(_conv_ffn, gain2=gains[2][None], gain3=gains[3][None], w_up=bf(w_ffn_up[l]),
                                conv_w=ffn_conv[l], w_down=bf(w_ffn_down[l]))
        xp = ffn(xp, mod_p, seq_len=lp, rows_per_mod=tp)
        xs = ffn(xs, mod_s, seq_len=ls, rows_per_mod=ls)
    st0, (k1, v1), (s2_re, s2_im), st3 = ctx_states
    return (xp.reshape(nbp, lp, D_MODEL), xs.reshape(nbs, ls, D_MODEL), st0, k1, v1, s2_re, s2_im, st3)
```

```python
import functools
import math

import jax
import jax.numpy as jnp
from jax import lax
from jax.experimental import pallas as pl
from jax.experimental.pallas import tpu as pltpu

F32 = jnp.float32
BF16 = jnp.bfloat16

D_MODEL = 1024
DEPTH = 4
GRID_W = 64
GDN_HEADS = 8
GDN_DK = 128
GDN_DV = 128
GDN_CHUNK = 64
DIFF_DH = 64
DIFF_HEADS = 8
ROPE_BASE = 10000.0
S5_GROUP_CH = 16
S5_GROUPS = 64
S5_STATE = 64
FFN_DIM = 2816
MOD_CHUNKS = 6
EPS = 1e-6

SUBLANES = 8
LANES = 128
VMEM_LIMIT_BYTES = 56 * 1024 * 1024

ROW_TILE = 1024
FFN_ROW_TILE = 512
FFN_CHUNK = 256
S5_CHUNK = 16
ATT_KEY_CHUNK = 256
ATT_QUERY_TILE = 256
ATT_TILES_PER_STEP = 4


def _cparams(*sem):
    return pltpu.CompilerParams(dimension_semantics=sem, vmem_limit_bytes=VMEM_LIMIT_BYTES)


def _silu(x):
    return x * jax.nn.sigmoid(x)


def _rms(x, gain):
    return x * lax.rsqrt(jnp.mean(x * x, axis=-1, keepdims=True) + EPS) * gain


def _mod_slice(mod_ref, idx):
    return mod_ref[0, :, idx * D_MODEL:(idx + 1) * D_MODEL]


def _rms_mod(x, gain, mod_ref, shift_idx, scale_idx):
    return _rms(x, gain) * (1.0 + _mod_slice(mod_ref, scale_idx)) + _mod_slice(mod_ref, shift_idx)


def _residual(x, y, gain, mod_ref, gate_idx):
    return x + _mod_slice(mod_ref, gate_idx) * _rms(y, gain)


def _dot(a, b):
    return jnp.dot(a.astype(BF16), b.astype(BF16), preferred_element_type=F32)


def _dot_nt(a, b):
    return lax.dot_general(a.astype(BF16), b.astype(BF16), (((1,), (1,)), ((), ())),
                           preferred_element_type=F32)


def _dot_tn(a, b):
    return lax.dot_general(a.astype(BF16), b.astype(BF16), (((0,), (0,)), ((), ())),
                           preferred_element_type=F32)


def _dot_f32(a, b):
    return jnp.dot(a, b, preferred_element_type=F32, precision=lax.Precision.HIGHEST)


def _mod_kernel(c_ref, w_ref, b_ref, o_ref):
    o_ref[0] = _dot(_silu(c_ref[...]), w_ref[0]) + b_ref[0]


def _modulation(cond, w_mod, b_mod):
    rows = cond.shape[0]
    n = w_mod.shape[-1]
    tn = 1536
    return pl.pallas_call(
        _mod_kernel,
        grid=(DEPTH, n // tn),
        in_specs=[pl.BlockSpec((rows, D_MODEL), lambda l, j: (0, 0)),
                  pl.BlockSpec((1, D_MODEL, tn), lambda l, j: (l, 0, j)),
                  pl.BlockSpec((1, 1, tn), lambda l, j: (l, 0, j))],
        out_specs=pl.BlockSpec((1, rows, tn), lambda l, j: (l, 0, j)),
        out_shape=jax.ShapeDtypeStruct((DEPTH, rows, n), F32),
        compiler_params=_cparams("parallel", "parallel"),
        name="modulation",
    )(cond, w_mod, b_mod.reshape(DEPTH, 1, n))


def _norm_linear_kernel(x_ref, mod_ref, g_ref, w_ref, o_ref, h_scr):
    @pl.when(pl.program_id(1) == 0)
    def _():
        h_scr[...] = _rms_mod(x_ref[...], g_ref[...], mod_ref, 0, 1).astype(BF16)

    o_ref[...] = jnp.dot(h_scr[...], w_ref[...].astype(BF16), preferred_element_type=F32)


def _mod_spec(tm, rows_per_mod):
    return pl.BlockSpec((1, 1, MOD_CHUNKS * D_MODEL), lambda i, *_: ((i * tm) // rows_per_mod, 0, 0))


def _norm_linear(x, mod, gain, w, rows_per_mod, tn):
    t = x.shape[0]
    n = w.shape[1]
    tm = min(ROW_TILE, rows_per_mod)
    return pl.pallas_call(
        _norm_linear_kernel,
        grid=(t // tm, n // tn),
        in_specs=[pl.BlockSpec((tm, D_MODEL), lambda i, j: (i, 0)),
                  _mod_spec(tm, rows_per_mod),
                  pl.BlockSpec((1, D_MODEL), lambda i, j: (0, 0)),
                  pl.BlockSpec((D_MODEL, tn), lambda i, j: (0, j))],
        out_specs=pl.BlockSpec((tm, tn), lambda i, j: (i, j)),
        out_shape=jax.ShapeDtypeStruct((t, n), F32),
        scratch_shapes=[pltpu.VMEM((tm, D_MODEL), BF16)],
        compiler_params=_cparams("parallel", "arbitrary"),
        name="norm_linear",
    )(x, mod, gain, w)


def _norm_mod_kernel(x_ref, mod_ref, g_ref, o_ref):
    o_ref[...] = _rms_mod(x_ref[...], g_ref[...], mod_ref, 0, 1)


def _norm_mod(x, mod, gain, rows_per_mod):
    t = x.shape[0]
    tm = min(ROW_TILE, rows_per_mod)
    return pl.pallas_call(
        _norm_mod_kernel,
        grid=(t // tm,),
        in_specs=[pl.BlockSpec((tm, D_MODEL), lambda i: (i, 0)),
                  _mod_spec(tm, rows_per_mod),
                  pl.BlockSpec((1, D_MODEL), lambda i: (0, 0))],
        out_specs=pl.BlockSpec((tm, D_MODEL), lambda i: (i, 0)),
        out_shape=jax.ShapeDtypeStruct((t, D_MODEL), F32),
        compiler_params=_cparams("parallel"),
        name="norm_mod",
    )(x, mod, gain)


def _plain_out_kernel(x_ref, y_ref, mod_ref, g_ref, w_ref, o_ref):
    y = _dot(y_ref[...], w_ref[...])
    o_ref[...] = _residual(x_ref[...], y, g_ref[...], mod_ref, 2)


def _gdn_out_kernel(x_ref, of_ref, ob_ref, gate_ref, hn_ref, mod_ref, g_ref, w_ref, o_ref, y_scr):
    o = of_ref[...] + ob_ref[...]
    gate = _silu(gate_ref[...])
    hn = hn_ref[...]
    for h in range(GDN_HEADS):
        sl = slice(h * GDN_DV, (h + 1) * GDN_DV)
        y_scr[:, sl] = (_rms(o[:, sl], hn) * gate[:, sl]).astype(BF16)
    y = jnp.dot(y_scr[...], w_ref[...].astype(BF16), preferred_element_type=F32)
    o_ref[...] = _residual(x_ref[...], y, g_ref[...], mod_ref, 2)


def _s5_out_kernel(x_ref, h_ref, ys_ref, d_ref, mod_ref, g_ref, w_ref, o_ref):
    y = jax.nn.gelu(d_ref[...] * h_ref[...] + ys_ref[...])
    z = _dot(y, w_ref[...])
    val = z[:, :D_MODEL]
    gate = z[:, D_MODEL:]
    o_ref[...] = _residual(x_ref[...], val * jax.nn.sigmoid(gate), g_ref[...], mod_ref, 2)


def _row_spec(tm, width=D_MODEL, col=0):
    return pl.BlockSpec((tm, width), lambda i: (i, col))


def _const_spec(shape):
    return pl.BlockSpec(shape, lambda i: (0,) * len(shape))


def _plain_out(x, y, mod, gain, w, rows_per_mod):
    t = x.shape[0]
    tm = min(ROW_TILE, rows_per_mod)
    return pl.pallas_call(
        _plain_out_kernel,
        grid=(t // tm,),
        in_specs=[_row_spec(tm), _row_spec(tm), _mod_spec(tm, rows_per_mod),
                  _const_spec((1, D_MODEL)), _const_spec((D_MODEL, D_MODEL))],
        out_specs=_row_spec(tm),
        out_shape=jax.ShapeDtypeStruct((t, D_MODEL), F32),
        compiler_params=_cparams("parallel"),
        name="plain_out",
    )(x, y, mod, gain, w)


def _gdn_out(x, o_f, o_b, proj, head_gain, mod, gain, w, rows_per_mod):
    t = x.shape[0]
    tm = min(ROW_TILE, rows_per_mod)
    return pl.pallas_call(
        _gdn_out_kernel,
        grid=(t // tm,),
        in_specs=[_row_spec(tm), _row_spec(tm), _row_spec(tm),
                  _row_spec(tm, D_MODEL, 3),
                  _const_spec((1, GDN_DV)), _mod_spec(tm, rows_per_mod),
                  _const_spec((1, D_MODEL)), _const_spec((D_MODEL, D_MODEL))],
        out_specs=_row_spec(tm),
        out_shape=jax.ShapeDtypeStruct((t, D_MODEL), F32),
        scratch_shapes=[pltpu.VMEM((tm, D_MODEL), BF16)],
        compiler_params=_cparams("parallel"),
        name="gdn_out",
    )(x, o_f, o_b, proj, head_gain, mod, gain, w)


def _s5_out(x, h, y_scan, d_skip, mod, gain, w, rows_per_mod):
    t = x.shape[0]
    tm = min(ROW_TILE // 2, rows_per_mod)
    return pl.pallas_call(
        _s5_out_kernel,
        grid=(t // tm,),
        in_specs=[_row_spec(tm), _row_spec(tm), _row_spec(tm), _const_spec((1, D_MODEL)),
                  _mod_spec(tm, rows_per_mod), _const_spec((1, D_MODEL)),
                  _const_spec((D_MODEL, 2 * D_MODEL))],
        out_specs=_row_spec(tm),
        out_shape=jax.ShapeDtypeStruct((t, D_MODEL), F32),
        compiler_params=_cparams("parallel"),
        name="s5_out",
    )(x, h, y_scan, d_skip, mod, gain, w)


def _ffn_kernel(x_ref, xp_ref, xn_ref, mod_ref, g2_ref, g3_ref, wu_ref, cw_ref, wd_ref, o_ref,
                h_scr, z_scr, act_scr, *, tm, seq_len):
    i = pl.program_id(0)
    ck = FFN_CHUNK
    g2 = g2_ref[...]
    keep_prev = jnp.where((i * tm) % seq_len == 0, 0.0, 1.0)
    keep_next = jnp.where(((i + 1) * tm) % seq_len == 0, 0.0, 1.0)
    h_scr[0:SUBLANES] = (_rms_mod(xp_ref[...], g2, mod_ref, 3, 4) * keep_prev).astype(BF16)
    h_scr[SUBLANES:SUBLANES + tm] = _rms_mod(x_ref[...], g2, mod_ref, 3, 4).astype(BF16)
    h_scr[SUBLANES + tm:] = (_rms_mod(xn_ref[...], g2, mod_ref, 3, 4) * keep_next).astype(BF16)
    h = h_scr[...]

    def conv(z_ref, cols, col0):
        cw = cw_ref[:, col0:col0 + ck]
        return (cw[0:1] * z_ref[SUBLANES - 1:SUBLANES - 1 + tm, cols]
                + cw[1:2] * z_ref[SUBLANES:SUBLANES + tm, cols]
                + cw[2:3] * z_ref[SUBLANES + 1:SUBLANES + 1 + tm, cols])

    for k in range(FFN_DIM // ck):
        z_ref = z_scr.at[k % 2]
        g0 = k * ck
        v0 = FFN_DIM + k * ck
        z_ref[:, :ck] = jnp.dot(h, wu_ref[:, g0:g0 + ck].astype(BF16), preferred_element_type=F32)
        z_ref[:, ck:] = jnp.dot(h, wu_ref[:, v0:v0 + ck].astype(BF16), preferred_element_type=F32)
        gate = conv(z_ref, slice(0, ck), g0)
        val = conv(z_ref, slice(ck, 2 * ck), v0)
        act_scr[:, g0:g0 + ck] = (_silu(gate) * val).astype(BF16)
    y = jnp.dot(act_scr[...], wd_ref[...].astype(BF16), preferred_element_type=F32)
    o_ref[...] = _residual(x_ref[...], y, g3_ref[...], mod_ref, 5)


def _conv_ffn(x, mod, gain2, gain3, w_up, conv_w, w_down, seq_len, rows_per_mod):
    t = x.shape[0]
    tm = min(FFN_ROW_TILE, seq_len)
    nblk8 = t // SUBLANES
    r8 = tm // SUBLANES
    resident = lambda shape: pl.BlockSpec(shape, lambda i: (0, 0), pipeline_mode=pl.Buffered(1))
    return pl.pallas_call(
        functools.partial(_ffn_kernel, tm=tm, seq_len=seq_len),
        grid=(t // tm,),
        in_specs=[
            pl.BlockSpec((tm, D_MODEL), lambda i: (i, 0)),
            pl.BlockSpec((SUBLANES, D_MODEL), lambda i: (jnp.maximum(i * r8 - 1, 0), 0)),
            pl.BlockSpec((SUBLANES, D_MODEL), lambda i: (jnp.minimum((i + 1) * r8, nblk8 - 1), 0)),
            _mod_spec(tm, rows_per_mod),
            _const_spec((1, D_MODEL)),
            _const_spec((1, D_MODEL)),
            resident((D_MODEL, 2 * FFN_DIM)),
            resident((3, 2 * FFN_DIM)),
            resident((FFN_DIM, D_MODEL)),
        ],
        out_specs=pl.BlockSpec((tm, D_MODEL), lambda i: (i, 0)),
        out_shape=jax.ShapeDtypeStruct((t, D_MODEL), F32),
        scratch_shapes=[pltpu.VMEM((tm + 2 * SUBLANES, D_MODEL), BF16),
                        pltpu.VMEM((2, tm + 2 * SUBLANES, 2 * FFN_CHUNK), F32),
                        pltpu.VMEM((tm, FFN_DIM), BF16)],
        compiler_params=_cparams("parallel"),
        name="conv_ffn",
    )(x, x, x, mod, gain2, gain3, w_up, conv_w, w_down)


ROPE_HALF = DIFF_DH // 4
LOG2E = 1.4426950408889634


def _rope(x, cos, sin_signed):
    lane = lax.broadcasted_iota(jnp.int32, x.shape, 1)
    first = (lane % (2 * ROPE_HALF)) < ROPE_HALF
    rot = jnp.where(first, pltpu.roll(x, LANES - ROPE_HALF, 1), pltpu.roll(x, ROPE_HALF, 1))
    return x * cos + rot * sin_signed


def _diff_attn_body(q_ref, k_ref, v_ref, ck_ref, cv_ref, cos_ref, sin_ref, lam_ref, subln_ref,
                    o_ref, k_scr, vt_scr, s_scr, *, n_cache, tq, lam_init):
    qi = pl.program_id(2)
    seq_len = k_ref.shape[0]
    n_keys = n_cache + seq_len
    rope = cos_ref is not None
    tr = 512 if seq_len % 512 == 0 else seq_len

    @pl.when(qi == 0)
    def _():
        k = k_ref[...]
        if rope:
            k = _rope(k, cos_ref[...], sin_ref[...])
        if n_cache:
            k_scr[0:n_cache] = ck_ref[0].astype(BF16)
            vt_scr[:, 0:n_cache] = cv_ref[0].T.astype(BF16)
        k_scr[n_cache:] = k.astype(BF16)
        for r in range(0, seq_len, tr):
            vt_scr[:, n_cache + r:n_cache + r + tr] = v_ref[r:r + tr, :].T.astype(BF16)

    n_tiles = q_ref.shape[0] // tq
    kc = ATT_KEY_CHUNK
    n_chunks = n_keys // kc
    part = lambda x, op: functools.reduce(op, [x[r:r + SUBLANES] for r in range(0, kc, SUBLANES)])
    lf = lam_ref[...]
    lam = (jnp.exp(jnp.sum(lf[0:1] * lf[1:2], axis=1, keepdims=True))
           - jnp.exp(jnp.sum(lf[2:3] * lf[3:4], axis=1, keepdims=True)) + lam_init)

    def load_q(t):
        q = q_ref[t * tq:(t + 1) * tq, :]
        if rope:
            r0 = pl.multiple_of((qi * n_tiles + t) * tq, tq)
            q = _rope(q, cos_ref[pl.ds(r0, tq), :], sin_ref[pl.ds(r0, tq), :])
        q = q * (DIFF_DH ** -0.5 * LOG2E)
        lane = lax.broadcasted_iota(jnp.int32, q.shape, 1)
        return (jnp.where(lane < DIFF_DH, q, 0.0).astype(BF16), jnp.where(lane >= DIFF_DH, q, 0.0).astype(BF16))

    def scores(t, j, qz, m):
        kch = k_scr[j * kc:(j + 1) * kc, :]
        for c in range(2):
            s = lax.dot_general(kch, qz[c], (((1,), (1,)), ((), ())), preferred_element_type=F32)
            s_scr[t % 2, c, j * kc:(j + 1) * kc, :] = s
            pm = part(s, jnp.maximum)
            m[c] = pm if m[c] is None else jnp.maximum(m[c], pm)

    def values(t, j, m, l, acc):
        vt = vt_scr[:, j * kc:(j + 1) * kc]
        for c in range(2):
            p = jnp.exp2(s_scr[t % 2, c, j * kc:(j + 1) * kc, :] - m[c])
            ps = part(p, jnp.add)
            pv = jnp.dot(vt, p.astype(BF16), preferred_element_type=F32)
            l[c] = ps if l[c] is None else l[c] + ps
            acc[c] = pv if acc[c] is None else acc[c] + pv

    def finish(t, l, acc):
        l = [jnp.sum(x, axis=0, keepdims=True) for x in l]
        o = acc[0] * (1.0 / l[0]) - lam * (acc[1] * (1.0 / l[1]))
        ms = jnp.mean(o * o, axis=0, keepdims=True)
        y = (o * lax.rsqrt(ms + EPS) * subln_ref[...]) * (1.0 - lam_init)
        o_ref[t * tq:(t + 1) * tq, :] = y.T

    m_prev = None
    for t in range(n_tiles + 1):
        qz = load_q(t) if t < n_tiles else None
        m, l, acc = [None, None], [None, None], [None, None]
        for j in range(n_chunks):
            if t < n_tiles:
                scores(t, j, qz, m)
            if t > 0:
                values(t - 1, j, m_prev, l, acc)
        if t > 0:
            finish(t - 1, l, acc)
        if t < n_tiles:
            m_prev = [jnp.max(x, axis=0, keepdims=True) for x in m]


def _diff_attn_cached_kernel(q_ref, k_ref, v_ref, ck_ref, cv_ref, cos_ref, sin_ref, lam_ref, subln_ref,
                             o_ref, k_scr, vt_scr, s_scr, **kw):
    _diff_attn_body(q_ref, k_ref, v_ref, ck_ref, cv_ref, cos_ref, sin_ref, lam_ref, subln_ref,
                    o_ref, k_scr, vt_scr, s_scr, **kw)


def _diff_attn_context_kernel(q_ref, k_ref, v_ref, lam_ref, subln_ref, o_ref, k_scr, vt_scr, s_scr, **kw):
    _diff_attn_body(q_ref, k_ref, v_ref, None, None, None, None, lam_ref, subln_ref,
                    o_ref, k_scr, vt_scr, s_scr, **kw)


def _diff_attention(qkv, lam_vecs, subln, lam_init, nb, seq_len, cache=None, rope_tables=None):
    tq = ATT_QUERY_TILE
    n_tiles = min(ATT_TILES_PER_STEP, seq_len // tq)
    tqs = n_tiles * tq
    nq = seq_len // tqs
    width = 2 * DIFF_DH
    n_cache = 0 if cache is None else cache[0].shape[1]
    n_keys = n_cache + seq_len
    in_specs = [pl.BlockSpec((tqs, width), lambda b, h, i: (b * nq + i, h)),
                pl.BlockSpec((seq_len, width), lambda b, h, i: (b, DIFF_HEADS + h)),
                pl.BlockSpec((seq_len, width), lambda b, h, i: (b, 2 * DIFF_HEADS + h))]
    args = [qkv, qkv, qkv]
    if cache is not None:
        in_specs += [pl.BlockSpec((1, n_cache, width), lambda b, h, i: (b, 0, h))] * 2
        in_specs += [pl.BlockSpec((seq_len, width), lambda b, h, i: (0, 0))] * 2
        args += [cache[0], cache[1], rope_tables[0], rope_tables[1]]
        kern = _diff_attn_cached_kernel
    else:
        kern = _diff_attn_context_kernel
    in_specs += [pl.BlockSpec((4, DIFF_DH), lambda b, h, i: (0, 0)),
                 pl.BlockSpec((width, 1), lambda b, h, i: (0, 0))]
    args += [lam_vecs, subln.reshape(width, 1)]
    return pl.pallas_call(
        functools.partial(kern, n_cache=n_cache, tq=tq, lam_init=lam_init),
        grid=(nb, DIFF_HEADS, nq),
        in_specs=in_specs,
        out_specs=pl.BlockSpec((tqs, width), lambda b, h, i: (b * nq + i, h)),
        out_shape=jax.ShapeDtypeStruct((nb * seq_len, D_MODEL), F32),
        scratch_shapes=[pltpu.VMEM((n_keys, width), BF16), pltpu.VMEM((width, n_keys), BF16),
                        pltpu.VMEM((min(n_tiles, 2), 2, n_keys, tq), F32)],
        compiler_params=_cparams("parallel", "parallel", "arbitrary"),
        name="diff_attention",
    )(*args)


def _rope_tables(n_tokens):
    rows = (jnp.arange(n_tokens, dtype=jnp.int32) // GRID_W).astype(F32)
    cols = (jnp.arange(n_tokens, dtype=jnp.int32) % GRID_W).astype(F32)
    n_freq = DIFF_DH // 4
    inv_freq = ROPE_BASE ** (-jnp.arange(n_freq, dtype=F32) / n_freq)
    ang_r = rows[:, None] * inv_freq
    ang_c = cols[:, None] * inv_freq
    ang = jnp.concatenate([ang_r, ang_r, ang_c, ang_c] * 2, axis=-1)
    sign = jnp.where((jnp.arange(2 * DIFF_DH) % (2 * ROPE_HALF)) < ROPE_HALF, -1.0, 1.0).astype(F32)
    return jnp.cos(ang), jnp.sin(ang) * sign


GDN_QKV = 3 * GDN_HEADS * GDN_DK
GDN_PROJ = GDN_QKV + D_MODEL + LANES
GDN_SEQS_PER_STEP = 2


def _cumsum_rows(x, reverse):
    n = x.shape[0]
    row = lax.broadcasted_iota(jnp.int32, x.shape, 0)
    k = 1
    while k < n:
        if reverse:
            x = x + jnp.where(row < n - k, pltpu.roll(x, n - k, 0), 0.0)
        else:
            x = x + jnp.where(row >= k, pltpu.roll(x, k, 0), 0.0)
        k *= 2
    return x


def _gdn_units(units, eye, blocks):
    c = GDN_CHUNK
    for n in units:
        n["decay"] = jnp.where(n["incl"], jnp.exp(jnp.where(n["incl"], n["gcol"] - n["grow"], 0.0)), 0.0)
        n["kb"] = n["k"] * n["bcol"]
        n["kt"] = n["k"].T
    for n in units:
        a = _dot(jnp.concatenate([n["kb"], n["q"]], axis=0), n["kt"])
        n["lower"] = jnp.where(n["strict"], a[:c] * n["decay"], 0.0)
        n["intra"] = a[c:] * n["decay"]
        n["t_off"] = -jnp.where(blocks[0], n["lower"], 0.0)
    for blk in blocks[1:]:
        for n in units:
            n["p"] = _dot(eye + n["t_off"], jnp.where(blk, n["lower"], 0.0))
        for n in units:
            n["t_off"] = n["t_off"] - _dot(n["p"], eye + n["t_off"])
    for n in units:
        egc = jnp.exp(n["gcol"])
        rhs = jnp.concatenate([n["v"] * n["bcol"], n["kb"] * egc], axis=1)
        sol = rhs + _dot(n["t_off"], rhs)
        n["u"] = sol[:, :GDN_DV]
        n["wq"] = jnp.concatenate([sol[:, GDN_DV:], n["q"] * egc], axis=0)
    for n in units:
        ws = _dot(n["wq"], n["s"])
        n["v_new"] = n["u"] - ws[:c]
        n["qs"] = ws[c:]
        n["ikt"] = jnp.concatenate([n["intra"], n["kt"] * jnp.exp(n["gtot"] - n["grow"])], axis=0)
    outs = []
    for n in units:
        r = _dot(n["ikt"], n["v_new"])
        outs.append((n["qs"] + r[:c], n["s"] * jnp.exp(n["gtot"]) + r[c:]))
    return outs


def _gdn_prep_kernel(m_ref, p_ref, n_ref, ab_ref, cw_ref, alog_ref, dtb_ref, qkv_ref, gate_ref, ext_scr,
                     *, tm, seq_len):
    i = pl.program_id(0)
    keep_prev = jnp.where((i * tm) % seq_len == 0, 0.0, 1.0)
    keep_next = jnp.where(((i + 1) * tm) % seq_len == 0, 0.0, 1.0)
    ext_scr[0:SUBLANES] = p_ref[...] * keep_prev
    ext_scr[SUBLANES:SUBLANES + tm] = m_ref[...]
    ext_scr[SUBLANES + tm:] = n_ref[...] * keep_next
    ones = jnp.ones((GDN_DK, GDN_DK), BF16)
    for j in range(3 * GDN_HEADS):
        cols = slice(j * GDN_DK, (j + 1) * GDN_DK)
        cw = cw_ref[:, cols]
        x = _silu(cw[0:1] * ext_scr[SUBLANES - 1:SUBLANES - 1 + tm, cols]
                  + cw[1:2] * ext_scr[SUBLANES:SUBLANES + tm, cols]
                  + cw[2:3] * ext_scr[SUBLANES + 1:SUBLANES + 1 + tm, cols])
        if j < 2 * GDN_HEADS:
            sq = x * x
            hi = sq.astype(BF16)
            lo = (sq - hi.astype(F32)).astype(BF16)
            ssq = jnp.dot(hi, ones, preferred_element_type=F32) + jnp.dot(lo, ones, preferred_element_type=F32)
            x = x * lax.rsqrt(ssq + EPS)
            if j < GDN_HEADS:
                x = x * (GDN_DK ** -0.5)
        qkv_ref[:, cols] = x
    ab = ab_ref[...]
    z = ab + dtb_ref[...]
    g = -jnp.exp(alog_ref[...]) * (jnp.maximum(z, 0.0) + jnp.log(1.0 + jnp.exp(-jnp.abs(z))))
    lane = lax.broadcasted_iota(jnp.int32, ab.shape, 1)
    gate_ref[...] = jnp.where(lane < 2 * GDN_HEADS, g, jax.nn.sigmoid(ab))


def _gdn_prep(proj, conv_w, a_log, dt_bias, seq_len):
    t = proj.shape[0]
    tm = min(FFN_ROW_TILE, seq_len)
    r8 = tm // SUBLANES
    nblk8 = t // SUBLANES
    ab_col = (GDN_QKV + D_MODEL) // LANES
    pad = lambda p: jnp.pad(p.reshape(1, 2 * GDN_HEADS), ((0, 0), (0, LANES - 2 * GDN_HEADS)))
    return pl.pallas_call(
        functools.partial(_gdn_prep_kernel, tm=tm, seq_len=seq_len),
        grid=(t // tm,),
        in_specs=[pl.BlockSpec((tm, GDN_QKV), lambda i: (i, 0)),
                  pl.BlockSpec((SUBLANES, GDN_QKV), lambda i: (jnp.maximum(i * r8 - 1, 0), 0)),
                  pl.BlockSpec((SUBLANES, GDN_QKV), lambda i: (jnp.minimum((i + 1) * r8, nblk8 - 1), 0)),
                  pl.BlockSpec((tm, LANES), lambda i: (i, ab_col)),
                  _const_spec((3, GDN_QKV)), _const_spec((1, LANES)), _const_spec((1, LANES))],
        out_specs=[pl.BlockSpec((tm, GDN_QKV), lambda i: (i, 0)), pl.BlockSpec((tm, LANES), lambda i: (i, 0))],
        out_shape=[jax.ShapeDtypeStruct((t, GDN_QKV), F32), jax.ShapeDtypeStruct((t, LANES), F32)],
        scratch_shapes=[pltpu.VMEM((tm + 2 * SUBLANES, GDN_QKV), F32)],
        compiler_params=_cparams("parallel"),
        name="gdn_prep",
    )(proj, proj, proj, proj, conv_w, pad(a_log), pad(dt_bias))


def _gdn_kernel(fq_ref, fg_ref, bq_ref, bg_ref, *rest, has_s0):
    if has_s0:
        s0_ref, of_ref, ob_ref, s_ref = rest
    else:
        of_ref, ob_ref, s_ref = rest
    c = pl.program_id(1)

    @pl.when(c == 0)
    def _():
        if has_s0:
            s_ref[...] = s0_ref[...]
        else:
            s_ref[...] = jnp.zeros_like(s_ref)

    ri = lax.broadcasted_iota(jnp.int32, (GDN_CHUNK, GDN_CHUNK), 0)
    ci = lax.broadcasted_iota(jnp.int32, (GDN_CHUNK, GDN_CHUNK), 1)
    eye = jnp.where(ri == ci, 1.0, 0.0)
    blocks = []
    size = 1
    while size < GDN_CHUNK:
        blocks.append((ri // (2 * size) == ci // (2 * size)) & (ri // size != ci // size))
        size *= 2

    units = []
    for b in range(GDN_SEQS_PER_STEP):
        for d, (q_ref, g_ref) in enumerate(((fq_ref, fg_ref), (bq_ref, bg_ref))):
            gates = g_ref[b]
            gc = _cumsum_rows(gates, reverse=(d == 1))
            gct = jnp.concatenate([gc, gc], axis=0).T
            if d == 0:
                incl, strict, last = ri >= ci, ri > ci, GDN_CHUNK - 1
            else:
                incl, strict, last = ri <= ci, ri < ci, 0
            for h in range(GDN_HEADS):
                col = d * GDN_HEADS + h
                units.append(dict(
                    q=q_ref[b, :, h * GDN_DK:(h + 1) * GDN_DK],
                    k=q_ref[b, :, (GDN_HEADS + h) * GDN_DK:(GDN_HEADS + h + 1) * GDN_DK],
                    v=q_ref[b, :, (2 * GDN_HEADS + h) * GDN_DK:(2 * GDN_HEADS + h + 1) * GDN_DK],
                    gcol=gc[:, col:col + 1], grow=gct[col:col + 1, 0:GDN_CHUNK],
                    bcol=gates[:, 2 * GDN_HEADS + col:2 * GDN_HEADS + col + 1],
                    gtot=gc[last:last + 1, col:col + 1], s=s_ref[b, d, h], incl=incl, strict=strict))

    outs = iter(_gdn_units(units, eye, blocks))
    for b in range(GDN_SEQS_PER_STEP):
        for d, o_ref in enumerate((of_ref, ob_ref)):
            for h in range(GDN_HEADS):
                o, s_new = next(outs)
                o_ref[b, :, h * GDN_DV:(h + 1) * GDN_DV] = o
                s_ref[b, d, h] = s_new


def _gdn_scan(qkv, gates, nb, seq_len, s0=None):
    n = seq_len // GDN_CHUNK
    g = GDN_SEQS_PER_STEP
    hv = GDN_HEADS * GDN_DV

    def chunk_specs(chunk_of):
        return [pl.BlockSpec((g, GDN_CHUNK, GDN_QKV), lambda b, c: (b, chunk_of(c), 0)),
                pl.BlockSpec((g, GDN_CHUNK, LANES), lambda b, c: (b, chunk_of(c), 0))]

    state_spec = pl.BlockSpec((g, 2, GDN_HEADS, GDN_DK, GDN_DV), lambda b, c: (b, 0, 0, 0, 0))
    in_specs = chunk_specs(lambda c: c) + chunk_specs(lambda c: n - 1 - c)
    qkv = qkv.reshape(nb, seq_len, GDN_QKV)
    gates = gates.reshape(nb, seq_len, LANES)
    args = [qkv, gates, qkv, gates]
    if s0 is not None:
        in_specs.append(state_spec)
        args.append(s0)
    o_f, o_b, s_fin = pl.pallas_call(
        functools.partial(_gdn_kernel, has_s0=s0 is not None),
        grid=(nb // g, n),
        in_specs=in_specs,
        out_specs=[pl.BlockSpec((g, GDN_CHUNK, hv), lambda b, c: (b, c, 0)),
                   pl.BlockSpec((g, GDN_CHUNK, hv), lambda b, c: (b, n - 1 - c, 0)),
                   state_spec],
        out_shape=[jax.ShapeDtypeStruct((nb, seq_len, hv), F32), jax.ShapeDtypeStruct((nb, seq_len, hv), F32),
                   jax.ShapeDtypeStruct((nb, 2, GDN_HEADS, GDN_DK, GDN_DV), F32)],
        compiler_params=_cparams("parallel", "arbitrary"),
        name="gdn_scan",
    )(*args)
    return o_f.reshape(nb * seq_len, hv), o_b.reshape(nb * seq_len, hv), s_fin


def _gdn_mixer(x, mod, gains, w_cat, conv_w, a_log, dt_bias, head_gain, w_out, nb, seq_len, rows_per_mod,
               s0=None):
    proj = _norm_linear(x, mod, gains[0][None], w_cat, rows_per_mod, LANES * 11)
    qkv, gates = _gdn_prep(proj, conv_w, a_log, dt_bias, seq_len)
    o_f, o_b, s_fin = _gdn_scan(qkv, gates, nb, seq_len, s0)
    x = _gdn_out(x, o_f, o_b, proj, head_gain[None], mod, gains[1][None], w_out, rows_per_mod)
    return x, s_fin


def _diff_mixer(x, mod, gains, w_qkv, lam_vecs, subln, w_out, lam_init, nb, seq_len, rows_per_mod,
                cache=None, rope_tables=None):
    qkv = _norm_linear(x, mod, gains[0][None], w_qkv, rows_per_mod, D_MODEL)
    heads = _diff_attention(qkv, lam_vecs, subln, lam_init, nb, seq_len, cache, rope_tables)
    x = _plain_out(x, heads, mod, gains[1][None], w_out, rows_per_mod)
    return x, qkv


S5_IN = S5_CHUNK * S5_GROUP_CH
S5_BITS = 5


def _cmul(ar, ai, br, bi):
    return ar * br - ai * bi, ar * bi + ai * br


def _discretize(a_re, a_im, log_dt):
    dt = jnp.exp(log_dt)
    mag = jnp.exp(dt * a_re)
    abar_re = mag * jnp.cos(dt * a_im)
    abar_im = mag * jnp.sin(dt * a_im)
    den = a_re * a_re + a_im * a_im
    f_re = ((abar_re - 1.0) * a_re + abar_im * a_im) / den
    f_im = (abar_im * a_re - (abar_re - 1.0) * a_im) / den
    return abar_re, abar_im, f_re, f_im


def _cpow(ar, ai, expo):
    pr = jnp.ones(expo.shape, F32)
    pi = jnp.zeros(expo.shape, F32)
    for b in range(S5_BITS):
        bit = ((expo >> b) & 1) == 1
        pr, pi = _cmul(pr, pi, jnp.where(bit, ar, 1.0), jnp.where(bit, ai, 0.0))
        if b + 1 < S5_BITS:
            ar, ai = _cmul(ar, ai, ar, ai)
    return pr, pi


def _s5_ops_kernel(arow_ref, acol_ref, bpc_ref, bt_ref, ccat_ref, csep_ref, mc_ref, ef_ref, a16_ref):
    arow = arow_ref[0]
    ar, ai, fr, fi = _discretize(arow[0:1], arow[1:2], arow[2:3])
    row = lax.broadcasted_iota(jnp.int32, (S5_IN, LANES), 0)
    lane = lax.broadcasted_iota(jnp.int32, (S5_IN, LANES), 1)
    tok = row // S5_GROUP_CH
    fwd = lane < S5_STATE
    tile = lambda x: jnp.concatenate([x] * S5_CHUNK, axis=0)

    pr, pi = _cpow(ar, ai, jnp.where(fwd, S5_CHUNK - 1 - tok, tok))
    bt = bt_ref[0]
    bbr, bbi = _cmul(fr, fi, bt[0], bt[1])
    er, ei = _cmul(pr, pi, tile(bbr), tile(bbi))
    ef_ref[0, 0] = er
    ef_ref[0, 1] = ei
    pr, pi = _cpow(ar, ai, jnp.where(fwd, tok + 1, S5_CHUNK - tok))
    cc = ccat_ref[0]
    gr, gi = _cmul(pr, pi, tile(cc[0]), tile(cc[1]))
    ef_ref[0, 2] = gr
    ef_ref[0, 3] = -gi
    p16r, p16i = _cpow(ar, ai, jnp.full((1, LANES), S5_CHUNK, jnp.int32))
    a16_ref[0, 0:1] = p16r
    a16_ref[0, 1:2] = p16i

    acol = acol_ref[0]
    lane_w = lax.broadcasted_iota(jnp.int32, (S5_GROUP_CH, S5_IN), 1)
    place = jnp.where(lane_w % S5_GROUP_CH == lax.broadcasted_iota(jnp.int32, (S5_GROUP_CH, S5_IN), 0), 1.0, 0.0)
    lag = lax.broadcasted_iota(jnp.int32, (S5_STATE, S5_IN), 1) // S5_GROUP_CH
    kw = []
    for d in range(2):
        cr, ci, gr_, gi_ = _discretize(acol[:, 3 * d:3 * d + 1], acol[:, 3 * d + 1:3 * d + 2],
                                       acol[:, 3 * d + 2:3 * d + 3])
        bbr, bbi = _cmul(gr_, gi_, bpc_ref[0, 2 * d], bpc_ref[0, 2 * d + 1])
        pr, pi = _cpow(jnp.broadcast_to(cr, lag.shape), jnp.broadcast_to(ci, lag.shape),
                       S5_CHUNK - 1 - lag if d == 0 else lag)
        hr, hi = _cmul(pr, pi, _dot_f32(bbr, place), _dot_f32(bbi, place))
        kw.append(_dot_f32(csep_ref[0, 2 * d], hr) - _dot_f32(csep_ref[0, 2 * d + 1], hi))
    for t in range(S5_CHUNK):
        sh = (S5_CHUNK - 1 - t) * S5_GROUP_CH
        m_f = jnp.where(lane_w < (t + 1) * S5_GROUP_CH, pltpu.roll(kw[0], (S5_IN - sh) % S5_IN, 1), 0.0)
        m_b = jnp.where(lane_w >= t * S5_GROUP_CH, pltpu.roll(kw[1], t * S5_GROUP_CH, 1), 0.0)
        mc_ref[0, t * S5_GROUP_CH:(t + 1) * S5_GROUP_CH, :] = m_f + m_b


def _s5_operators(a_re, a_im, log_dt, b_re, b_im, c_re, c_im):
    g = S5_GROUPS
    cat = lambda x: jnp.concatenate([x[0], x[1]], axis=-1)
    dt_row = jnp.broadcast_to(log_dt[:, :, None], (2, g, S5_STATE))
    arow = jnp.stack([cat(a_re), cat(a_im), cat(dt_row)], axis=1)
    acol = jnp.stack([a_re[0], a_im[0], dt_row[0], a_re[1], a_im[1], dt_row[1]], axis=-1)
    bpc = jnp.stack([b_re[0], b_im[0], b_re[1], b_im[1]], axis=1)
    swap = lambda x: jnp.swapaxes(x, -1, -2)
    bt = jnp.stack([cat(swap(b_re)), cat(swap(b_im))], axis=1)
    ccat = jnp.stack([cat(c_re), cat(c_im)], axis=1)
    csep = jnp.stack([c_re[0], c_im[0], c_re[1], c_im[1]], axis=1)
    spec = lambda *s: pl.BlockSpec((1,) + s, lambda i: (i,) + (0,) * len(s))
    return pl.pallas_call(
        _s5_ops_kernel,
        grid=(g,),
        in_specs=[spec(3, LANES), spec(S5_STATE, 6), spec(4, S5_STATE, S5_GROUP_CH),
                  spec(2, S5_GROUP_CH, LANES), spec(2, S5_GROUP_CH, LANES), spec(4, S5_GROUP_CH, S5_STATE)],
        out_specs=[spec(S5_IN, S5_IN), spec(4, S5_IN, LANES), spec(2, LANES)],
        out_shape=[jax.ShapeDtypeStruct((g, S5_IN, S5_IN), F32),
                   jax.ShapeDtypeStruct((g, 4, S5_IN, LANES), F32),
                   jax.ShapeDtypeStruct((g, 2, LANES), F32)],
        compiler_params=_cparams("parallel"),
        name="s5_operators",
    )(arow, acol, bpc, bt, ccat, csep)


def _s5_scan_kernel(u_ref, mc_ref, ef_ref, a16_ref, *rest, nb, has_x0):
    if has_x0:
        x0r_ref, x0i_ref, y_ref, fr_ref, fi_ref, xr_scr, xi_scr = rest
    else:
        y_ref, fr_ref, fi_ref, xr_scr, xi_scr = rest
    u = u_ref[0].astype(BF16)
    rows = u.shape[0]
    nc = rows // nb
    vr = jnp.dot(u, ef_ref[0, 0].astype(BF16), preferred_element_type=F32)
    vi = jnp.dot(u, ef_ref[0, 1].astype(BF16), preferred_element_type=F32)
    row_c = lax.broadcasted_iota(jnp.int32, (rows, LANES), 0) % nc
    fwd = lax.broadcasted_iota(jnp.int32, (rows, LANES), 1) < S5_STATE
    pr = a16_ref[0, 0:1]
    pi = a16_ref[0, 1:2]
    if has_x0:
        expand = lambda x: jnp.broadcast_to(x[:, None, :], (nb, nc, LANES)).reshape(rows, LANES)
        x0r, x0i = expand(x0r_ref[0]), expand(x0i_ref[0])
        first = row_c == jnp.where(fwd, 0, nc - 1)
        ir, ii = _cmul(pr, pi, x0r, x0i)
        vr = vr + jnp.where(first, ir, 0.0)
        vi = vi + jnp.where(first, ii, 0.0)

    def shifted(x, k):
        down = jnp.where(row_c >= k, pltpu.roll(x, k, 0), 0.0)
        up = jnp.where(row_c < nc - k, pltpu.roll(x, rows - k, 0), 0.0)
        return jnp.where(fwd, down, up)

    k = 1
    while k < nc:
        sr, si = _cmul(pr, pi, shifted(vr, k), shifted(vi, k))
        vr, vi = vr + sr, vi + si
        pr, pi = _cmul(pr, pi, pr, pi)
        k *= 2
    er, ei = shifted(vr, 1), shifted(vi, 1)
    if has_x0:
        er = jnp.where(first, x0r, er)
        ei = jnp.where(first, x0i, ei)
    y = lax.dot_general(u, mc_ref[0].astype(BF16), (((1,), (1,)), ((), ())), preferred_element_type=F32)
    y = y + _dot_nt(er, ef_ref[0, 2]) + _dot_nt(ei, ef_ref[0, 3])
    y_ref[0] = y
    xr_scr[...] = vr
    xi_scr[...] = vi
    lane_b = lax.broadcasted_iota(jnp.int32, (nb, LANES), 1) < S5_STATE
    fr_ref[0] = jnp.where(lane_b, xr_scr[pl.ds(nc - 1, nb, stride=nc), :], xr_scr[pl.ds(0, nb, stride=nc), :])
    fi_ref[0] = jnp.where(lane_b, xi_scr[pl.ds(nc - 1, nb, stride=nc), :], xi_scr[pl.ds(0, nb, stride=nc), :])


def _s5_scan(u, ops, nb, x0=None):
    mc, ef, a16 = ops
    g, rows, _ = u.shape
    spec = lambda *s: pl.BlockSpec((1,) + s, lambda i: (i,) + (0,) * len(s))
    in_specs = [spec(rows, S5_IN), spec(S5_IN, S5_IN), spec(4, S5_IN, LANES), spec(2, LANES)]
    args = [u, mc, ef, a16]
    if x0 is not None:
        in_specs += [spec(nb, LANES), spec(nb, LANES)]
        args += list(x0)
    return pl.pallas_call(
        functools.partial(_s5_scan_kernel, nb=nb, has_x0=x0 is not None),
        grid=(g,),
        in_specs=in_specs,
        out_specs=[spec(rows, S5_IN), spec(nb, LANES), spec(nb, LANES)],
        out_shape=[jax.ShapeDtypeStruct((g, rows, S5_IN), F32),
                   jax.ShapeDtypeStruct((g, nb, LANES), F32),
                   jax.ShapeDtypeStruct((g, nb, LANES), F32)],
        scratch_shapes=[pltpu.VMEM((rows, LANES), F32), pltpu.VMEM((rows, LANES), F32)],
        compiler_params=_cparams("parallel"),
        name="s5_scan",
    )(*args)


def _to_chunk_major(h, nb, seq_len):
    nc = seq_len // S5_CHUNK
    h = h.reshape(nb * nc, S5_CHUNK, S5_GROUPS, S5_GROUP_CH)
    return jnp.transpose(h, (2, 0, 1, 3)).reshape(S5_GROUPS, nb * nc, S5_IN)


def _from_chunk_major(y, nb, seq_len):
    nc = seq_len // S5_CHUNK
    y = y.reshape(S5_GROUPS, nb * nc, S5_CHUNK, S5_GROUP_CH)
    return jnp.transpose(y, (1, 2, 0, 3)).reshape(nb * seq_len, D_MODEL)


def _s5_mixer(x, mod, gains, ops, d_skip, w_glu, nb, seq_len, rows_per_mod, cache=None):
    h = _norm_mod(x, mod, gains[0][None], rows_per_mod)
    x0 = None
    if cache is not None:
        x0 = [jnp.transpose(s, (2, 0, 1, 3)).reshape(S5_GROUPS, nb, 2 * S5_STATE) for s in cache]
    y, fin_re, fin_im = _s5_scan(_to_chunk_major(h, nb, seq_len), ops, nb, x0)
    x = _s5_out(x, h, _from_chunk_major(y, nb, seq_len), d_skip[None], mod, gains[1][None], w_glu, rows_per_mod)
    unpack = lambda f: jnp.transpose(f.reshape(S5_GROUPS, nb, 2, S5_STATE), (1, 2, 0, 3))
    return x, (unpack(fin_re), unpack(fin_im))


def kernel(x_prompt, x_sample, state_l0_gdn, cache_l1_k, cache_l1_v, state_l2_s5_re, state_l2_s5_im, state_l3_gdn, c, c_ctx, w_mod, b_mod, norm_gain, w_ffn_up, ffn_conv, w_ffn_down, w_gdn_qkv, gdn_conv, w_gdn_gate, w_gdn_alpha, w_gdn_beta, gdn_a_log, gdn_dt_bias, gdn_norm, w_gdn_out, w_diff_qkv, diff_lam, diff_subln, w_diff_out, s5_a_re, s5_a_im, s5_log_dt, s5_b_re, s5_b_im, s5_c_re, s5_c_im, s5_d, w_s5_glu):
    nbp, lp = x_prompt.shape[:2]
    nbs, ls = x_sample.shape[:2]
    tp, ts = nbp * lp, nbs * ls
    pad_rows = -(nbs + 1) % SUBLANES
    cond = jnp.concatenate([c, c_ctx[None], jnp.zeros((pad_rows, D_MODEL), F32)], axis=0)
    mod_all = _modulation(cond, w_mod, b_mod)
    s5_ops = _s5_operators(s5_a_re[0], s5_a_im[0], s5_log_dt[0], s5_b_re[0], s5_b_im[0], s5_c_re[0], s5_c_im[0])
    rope = _rope_tables(ls)
    past = cache_l1_k.shape[1]
    attn_cache = (cache_l1_k.reshape(nbs, past, D_MODEL), cache_l1_v.reshape(nbs, past, D_MODEL))
    gdn_caches = (state_l0_gdn, state_l3_gdn)
    bf = lambda w: w.astype(BF16)

    xp = x_prompt.reshape(tp, D_MODEL)
    xs = x_sample.reshape(ts, D_MODEL)
    ctx_states = []
    for l in range(DEPTH):
        kind, j = l % 3, l // 3
        gains = norm_gain[l]
        mod_s = mod_all[l, :nbs][:, None, :]
        mod_p = mod_all[l, nbs:nbs + 1][:, None, :]
        if kind == 0:
            w_cat = jnp.concatenate([w_gdn_qkv[j], w_gdn_gate[j], w_gdn_alpha[j], w_gdn_beta[j],
                                     jnp.zeros((D_MODEL, LANES - 4 * GDN_HEADS), F32)], axis=1).astype(BF16)
            gdn = functools.partial(_gdn_mixer, gains=gains, w_cat=w_cat, conv_w=gdn_conv[j], a_log=gdn_a_log[j],
                                    dt_bias=gdn_dt_bias[j], head_gain=gdn_norm[j], w_out=bf(w_gdn_out[j]))
            xp, st = gdn(xp, mod_p, nb=nbp, seq_len=lp, rows_per_mod=tp)
            xs, _ = gdn(xs, mod_s, nb=nbs, seq_len=ls, rows_per_mod=ls, s0=gdn_caches[j])
            ctx_states.append(st)
        elif kind == 1:
            lam_init = 0.8 - 0.6 * math.exp(-0.3 * l)
            attn = functools.partial(_diff_mixer, gains=gains, w_qkv=bf(w_diff_qkv[j]), lam_vecs=diff_lam[j],
                                     subln=diff_subln[j], w_out=bf(w_diff_out[j]), lam_init=lam_init)
            xp, qkv = attn(xp, mod_p, nb=nbp, seq_len=lp, rows_per_mod=tp)
            xs, _ = attn(xs, mod_s, nb=nbs, seq_len=ls, rows_per_mod=ls, cache=attn_cache, rope_tables=rope)
            ctx_states.append((qkv[:, D_MODEL:2 * D_MODEL].reshape(nbp, lp, DIFF_HEADS, 2, DIFF_DH),
                               qkv[:, 2 * D_MODEL:].reshape(nbp, lp, DIFF_HEADS, 2 * DIFF_DH)))
        else:
            s5 = functools.partial(_s5_mixer, gains=gains, ops=s5_ops, d_skip=s5_d[j], w_glu=bf(w_s5_glu[j]))
            xp, st = s5(xp, mod_p, nb=nbp, seq_len=lp, rows_per_mod=tp)
            xs, _ = s5(xs, mod_s, nb=nbs, seq_len=ls, rows_per_mod=ls, cache=(state_l2_s5_re, state_l2_s5_im))
            ctx_states.append(st)
        ffn = functools.partial(_conv_ffn, gain2=gains[2][None], gain3=gains[3][None], w_up=bf(w_ffn_up[l]),
                                conv_w=ffn_conv[l], w_down=bf(w_ffn_down[l]))
        xp = ffn(xp, mod_p, seq_len=lp, rows_per_mod=tp)
        xs = ffn(xs, mod_s, seq_len=ls, rows_per_mod=ls)
    st0, (k1, v1), (s2_re, s2_im), st3 = ctx_states
    return (xp.reshape(nbp, lp, D_MODEL), xs.reshape(nbs, ls, D_MODEL), st0, k1, v1, s2_re, s2_im, st3)
```

```python
import functools
import math

import jax
import jax.numpy as jnp
from jax import lax
from jax.experimental import pallas as pl
from jax.experimental.pallas import tpu as pltpu

F32 = jnp.float32
BF16 = jnp.bfloat16

D_MODEL = 1024
DEPTH = 4
GRID_W = 64
GDN_HEADS = 8
GDN_DK = 128
GDN_DV = 128
GDN_CHUNK = 64
DIFF_DH = 64
DIFF_HEADS = 8
ROPE_BASE = 10000.0
S5_GROUP_CH = 16
S5_GROUPS = 64
S5_STATE = 64
FFN_DIM = 2816
MOD_CHUNKS = 6
EPS = 1e-6

SUBLANES = 8
LANES = 128
VMEM_LIMIT_BYTES = 56 * 1024 * 1024

ROW_TILE = 1024
FFN_ROW_TILE = 512
FFN_CHUNK = 256
S5_CHUNK = 16
ATT_KEY_CHUNK = 256
ATT_QUERY_TILE = 256
ATT_TILES_PER_STEP = 16


def _cparams(*sem):
    return pltpu.CompilerParams(dimension_semantics=sem, vmem_limit_bytes=VMEM_LIMIT_BYTES)


def _silu(x):
    return x * jax.nn.sigmoid(x)


def _rms(x, gain):
    return x * lax.rsqrt(jnp.mean(x * x, axis=-1, keepdims=True) + EPS) * gain


def _mod_slice(mod_ref, idx):
    return mod_ref[0, :, idx * D_MODEL:(idx + 1) * D_MODEL]


def _rms_mod(x, gain, mod_ref, shift_idx, scale_idx):
    return _rms(x, gain) * (1.0 + _mod_slice(mod_ref, scale_idx)) + _mod_slice(mod_ref, shift_idx)


def _residual(x, y, gain, mod_ref, gate_idx):
    return x + _mod_slice(mod_ref, gate_idx) * _rms(y, gain)


def _dot(a, b):
    return jnp.dot(a.astype(BF16), b.astype(BF16), preferred_element_type=F32)


def _dot_nt(a, b):
    return lax.dot_general(a.astype(BF16), b.astype(BF16), (((1,), (1,)), ((), ())),
                           preferred_element_type=F32)


def _dot_tn(a, b):
    return lax.dot_general(a.astype(BF16), b.astype(BF16), (((0,), (0,)), ((), ())),
                           preferred_element_type=F32)


def _dot_f32(a, b):
    return jnp.dot(a, b, preferred_element_type=F32, precision=lax.Precision.HIGHEST)


def _mod_kernel(c_ref, w_ref, b_ref, o_ref):
    o_ref[0] = _dot(_silu(c_ref[...]), w_ref[0]) + b_ref[0]


def _modulation(cond, w_mod, b_mod):
    rows = cond.shape[0]
    n = w_mod.shape[-1]
    tn = 1536
    return pl.pallas_call(
        _mod_kernel,
        grid=(DEPTH, n // tn),
        in_specs=[pl.BlockSpec((rows, D_MODEL), lambda l, j: (0, 0)),
                  pl.BlockSpec((1, D_MODEL, tn), lambda l, j: (l, 0, j)),
                  pl.BlockSpec((1, 1, tn), lambda l, j: (l, 0, j))],
        out_specs=pl.BlockSpec((1, rows, tn), lambda l, j: (l, 0, j)),
        out_shape=jax.ShapeDtypeStruct((DEPTH, rows, n), F32),
        compiler_params=_cparams("parallel", "parallel"),
        name="modulation",
    )(cond, w_mod, b_mod.reshape(DEPTH, 1, n))


def _norm_linear_kernel(x_ref, mod_ref, g_ref, w_ref, o_ref, h_scr):
    @pl.when(pl.program_id(1) == 0)
    def _():
        h_scr[...] = _rms_mod(x_ref[...], g_ref[...], mod_ref, 0, 1).astype(BF16)

    o_ref[...] = jnp.dot(h_scr[...], w_ref[...].astype(BF16), preferred_element_type=F32)


def _mod_spec(tm, rows_per_mod):
    return pl.BlockSpec((1, 1, MOD_CHUNKS * D_MODEL), lambda i, *_: ((i * tm) // rows_per_mod, 0, 0))


def _norm_linear(x, mod, gain, w, rows_per_mod, tn):
    t = x.shape[0]
    n = w.shape[1]
    tm = min(ROW_TILE, rows_per_mod)
    return pl.pallas_call(
        _norm_linear_kernel,
        grid=(t // tm, n // tn),
        in_specs=[pl.BlockSpec((tm, D_MODEL), lambda i, j: (i, 0)),
                  _mod_spec(tm, rows_per_mod),
                  pl.BlockSpec((1, D_MODEL), lambda i, j: (0, 0)),
                  pl.BlockSpec((D_MODEL, tn), lambda i, j: (0, j))],
        out_specs=pl.BlockSpec((tm, tn), lambda i, j: (i, j)),
        out_shape=jax.ShapeDtypeStruct((t, n), F32),
        scratch_shapes=[pltpu.VMEM((tm, D_MODEL), BF16)],
        compiler_params=_cparams("parallel", "arbitrary"),
        name="norm_linear",
    )(x, mod, gain, w)


def _norm_mod_kernel(x_ref, mod_ref, g_ref, o_ref):
    o_ref[...] = _rms_mod(x_ref[...], g_ref[...], mod_ref, 0, 1)


def _norm_mod(x, mod, gain, rows_per_mod):
    t = x.shape[0]
    tm = min(ROW_TILE, rows_per_mod)
    return pl.pallas_call(
        _norm_mod_kernel,
        grid=(t // tm,),
        in_specs=[pl.BlockSpec((tm, D_MODEL), lambda i: (i, 0)),
                  _mod_spec(tm, rows_per_mod),
                  pl.BlockSpec((1, D_MODEL), lambda i: (0, 0))],
        out_specs=pl.BlockSpec((tm, D_MODEL), lambda i: (i, 0)),
        out_shape=jax.ShapeDtypeStruct((t, D_MODEL), F32),
        compiler_params=_cparams("parallel"),
        name="norm_mod",
    )(x, mod, gain)


def _plain_out_kernel(x_ref, y_ref, mod_ref, g_ref, w_ref, o_ref):
    y = _dot(y_ref[...], w_ref[...])
    o_ref[...] = _residual(x_ref[...], y, g_ref[...], mod_ref, 2)


def _gdn_out_kernel(x_ref, of_ref, ob_ref, gate_ref, hn_ref, mod_ref, g_ref, w_ref, o_ref, y_scr):
    o = of_ref[...] + ob_ref[...]
    gate = _silu(gate_ref[...])
    hn = hn_ref[...]
    for h in range(GDN_HEADS):
        sl = slice(h * GDN_DV, (h + 1) * GDN_DV)
        y_scr[:, sl] = (_rms(o[:, sl], hn) * gate[:, sl]).astype(BF16)
    y = jnp.dot(y_scr[...], w_ref[...].astype(BF16), preferred_element_type=F32)
    o_ref[...] = _residual(x_ref[...], y, g_ref[...], mod_ref, 2)


def _s5_out_kernel(x_ref, h_ref, ys_ref, d_ref, mod_ref, g_ref, w_ref, o_ref):
    y = jax.nn.gelu(d_ref[...] * h_ref[...] + ys_ref[...])
    z = _dot(y, w_ref[...])
    val = z[:, :D_MODEL]
    gate = z[:, D_MODEL:]
    o_ref[...] = _residual(x_ref[...], val * jax.nn.sigmoid(gate), g_ref[...], mod_ref, 2)


def _row_spec(tm, width=D_MODEL, col=0):
    return pl.BlockSpec((tm, width), lambda i: (i, col))


def _const_spec(shape):
    return pl.BlockSpec(shape, lambda i: (0,) * len(shape))


def _plain_out(x, y, mod, gain, w, rows_per_mod):
    t = x.shape[0]
    tm = min(ROW_TILE, rows_per_mod)
    return pl.pallas_call(
        _plain_out_kernel,
        grid=(t // tm,),
        in_specs=[_row_spec(tm), _row_spec(tm), _mod_spec(tm, rows_per_mod),
                  _const_spec((1, D_MODEL)), _const_spec((D_MODEL, D_MODEL))],
        out_specs=_row_spec(tm),
        out_shape=jax.ShapeDtypeStruct((t, D_MODEL), F32),
        compiler_params=_cparams("parallel"),
        name="plain_out",
    )(x, y, mod, gain, w)


def _gdn_out(x, o_f, o_b, proj, head_gain, mod, gain, w, rows_per_mod):
    t = x.shape[0]
    tm = min(ROW_TILE, rows_per_mod)
    return pl.pallas_call(
        _gdn_out_kernel,
        grid=(t // tm,),
        in_specs=[_row_spec(tm), _row_spec(tm), _row_spec(tm),
                  _row_spec(tm, D_MODEL, 3),
                  _const_spec((1, GDN_DV)), _mod_spec(tm, rows_per_mod),
                  _const_spec((1, D_MODEL)), _const_spec((D_MODEL, D_MODEL))],
        out_specs=_row_spec(tm),
        out_shape=jax.ShapeDtypeStruct((t, D_MODEL), F32),
        scratch_shapes=[pltpu.VMEM((tm, D_MODEL), BF16)],
        compiler_params=_cparams("parallel"),
        name="gdn_out",
    )(x, o_f, o_b, proj, head_gain, mod, gain, w)


def _s5_out(x, h, y_scan, d_skip, mod, gain, w, rows_per_mod):
    t = x.shape[0]
    tm = min(ROW_TILE // 2, rows_per_mod)
    return pl.pallas_call(
        _s5_out_kernel,
        grid=(t // tm,),
        in_specs=[_row_spec(tm), _row_spec(tm), _row_spec(tm), _const_spec((1, D_MODEL)),
                  _mod_spec(tm, rows_per_mod), _const_spec((1, D_MODEL)),
                  _const_spec((D_MODEL, 2 * D_MODEL))],
        out_specs=_row_spec(tm),
        out_shape=jax.ShapeDtypeStruct((t, D_MODEL), F32),
        compiler_params=_cparams("parallel"),
        name="s5_out",
    )(x, h, y_scan, d_skip, mod, gain, w)


def _ffn_kernel(x_ref, xp_ref, xn_ref, mod_ref, g2_ref, g3_ref, wu_ref, cw_ref, wd_ref, o_ref,
                h_scr, z_scr, act_scr, *, tm, seq_len):
    i = pl.program_id(0)
    ck = FFN_CHUNK
    g2 = g2_ref[...]
    keep_prev = jnp.where((i * tm) % seq_len == 0, 0.0, 1.0)
    keep_next = jnp.where(((i + 1) * tm) % seq_len == 0, 0.0, 1.0)
    h_scr[0:SUBLANES] = (_rms_mod(xp_ref[...], g2, mod_ref, 3, 4) * keep_prev).astype(BF16)
    h_scr[SUBLANES:SUBLANES + tm] = _rms_mod(x_ref[...], g2, mod_ref, 3, 4).astype(BF16)
    h_scr[SUBLANES + tm:] = (_rms_mod(xn_ref[...], g2, mod_ref, 3, 4) * keep_next).astype(BF16)
    h = h_scr[...]

    def conv(z_ref, cols, col0):
        cw = cw_ref[:, col0:col0 + ck]
        return (cw[0:1] * z_ref[SUBLANES - 1:SUBLANES - 1 + tm, cols]
                + cw[1:2] * z_ref[SUBLANES:SUBLANES + tm, cols]
                + cw[2:3] * z_ref[SUBLANES + 1:SUBLANES + 1 + tm, cols])

    for k in range(FFN_DIM // ck):
        z_ref = z_scr.at[k % 2]
        g0 = k * ck
        v0 = FFN_DIM + k * ck
        z_ref[:, :ck] = jnp.dot(h, wu_ref[:, g0:g0 + ck].astype(BF16), preferred_element_type=F32)
        z_ref[:, ck:] = jnp.dot(h, wu_ref[:, v0:v0 + ck].astype(BF16), preferred_element_type=F32)
        gate = conv(z_ref, slice(0, ck), g0)
        val = conv(z_ref, slice(ck, 2 * ck), v0)
        act_scr[:, g0:g0 + ck] = (_silu(gate) * val).astype(BF16)
    y = jnp.dot(act_scr[...], wd_ref[...].astype(BF16), preferred_element_type=F32)
    o_ref[...] = _residual(x_ref[...], y, g3_ref[...], mod_ref, 5)


def _conv_ffn(x, mod, gain2, gain3, w_up, conv_w, w_down, seq_len, rows_per_mod):
    t = x.shape[0]
    tm = min(FFN_ROW_TILE, seq_len)
    nblk8 = t // SUBLANES
    r8 = tm // SUBLANES
    resident = lambda shape: pl.BlockSpec(shape, lambda i: (0, 0), pipeline_mode=pl.Buffered(1))
    return pl.pallas_call(
        functools.partial(_ffn_kernel, tm=tm, seq_len=seq_len),
        grid=(t // tm,),
        in_specs=[
            pl.BlockSpec((tm, D_MODEL), lambda i: (i, 0)),
            pl.BlockSpec((SUBLANES, D_MODEL), lambda i: (jnp.maximum(i * r8 - 1, 0), 0)),
            pl.BlockSpec((SUBLANES, D_MODEL), lambda i: (jnp.minimum((i + 1) * r8, nblk8 - 1), 0)),
            _mod_spec(tm, rows_per_mod),
            _const_spec((1, D_MODEL)),
            _const_spec((1, D_MODEL)),
            resident((D_MODEL, 2 * FFN_DIM)),
            resident((3, 2 * FFN_DIM)),
            resident((FFN_DIM, D_MODEL)),
        ],
        out_specs=pl.BlockSpec((tm, D_MODEL), lambda i: (i, 0)),
        out_shape=jax.ShapeDtypeStruct((t, D_MODEL), F32),
        scratch_shapes=[pltpu.VMEM((tm + 2 * SUBLANES, D_MODEL), BF16),
                        pltpu.VMEM((2, tm + 2 * SUBLANES, 2 * FFN_CHUNK), F32),
                        pltpu.VMEM((tm, FFN_DIM), BF16)],
        compiler_params=_cparams("parallel"),
        name="conv_ffn",
    )(x, x, x, mod, gain2, gain3, w_up, conv_w, w_down)


ROPE_HALF = DIFF_DH // 4
LOG2E = 1.4426950408889634


def _rope(x, cos, sin_signed):
    lane = lax.broadcasted_iota(jnp.int32, x.shape, 1)
    first = (lane % (2 * ROPE_HALF)) < ROPE_HALF
    rot = jnp.where(first, pltpu.roll(x, LANES - ROPE_HALF, 1), pltpu.roll(x, ROPE_HALF, 1))
    return x * cos + rot * sin_signed


def _diff_attn_body(q_ref, k_ref, v_ref, ck_ref, cv_ref, cos_ref, sin_ref, lam_ref, subln_ref,
                    o_ref, k_scr, vt_scr, s_scr, *, n_cache, tq, lam_init):
    qi = pl.program_id(2)
    seq_len = k_ref.shape[0]
    n_keys = n_cache + seq_len
    rope = cos_ref is not None
    tr = 512 if seq_len % 512 == 0 else seq_len

    @pl.when(qi == 0)
    def _():
        k = k_ref[...]
        if rope:
            k = _rope(k, cos_ref[...], sin_ref[...])
        if n_cache:
            k_scr[0:n_cache] = ck_ref[0].astype(BF16)
            vt_scr[:, 0:n_cache] = cv_ref[0].T.astype(BF16)
        k_scr[n_cache:] = k.astype(BF16)
        for r in range(0, seq_len, tr):
            vt_scr[:, n_cache + r:n_cache + r + tr] = v_ref[r:r + tr, :].T.astype(BF16)

    n_tiles = q_ref.shape[0] // tq
    kc = ATT_KEY_CHUNK
    n_chunks = n_keys // kc
    part = lambda x, op: functools.reduce(op, [x[r:r + SUBLANES] for r in range(0, kc, SUBLANES)])
    lf = lam_ref[...]
    lam = (jnp.exp(jnp.sum(lf[0:1] * lf[1:2], axis=1, keepdims=True))
           - jnp.exp(jnp.sum(lf[2:3] * lf[3:4], axis=1, keepdims=True)) + lam_init)

    def tile_rows(t):
        start = t * tq
        return pl.ds(start if isinstance(start, int) else pl.multiple_of(start, tq), tq)

    def load_q(t):
        q = q_ref[tile_rows(t), :]
        if rope:
            g0 = pl.multiple_of((qi * n_tiles + t) * tq, tq)
            q = _rope(q, cos_ref[pl.ds(g0, tq), :], sin_ref[pl.ds(g0, tq), :])
        q = q * (DIFF_DH ** -0.5 * LOG2E)
        lane = lax.broadcasted_iota(jnp.int32, q.shape, 1)
        return (jnp.where(lane < DIFF_DH, q, 0.0).astype(BF16), jnp.where(lane >= DIFF_DH, q, 0.0).astype(BF16))

    def scores(slot, j, qz, m):
        kch = k_scr[j * kc:(j + 1) * kc, :]
        for c in range(2):
            s = lax.dot_general(kch, qz[c], (((1,), (1,)), ((), ())), preferred_element_type=F32)
            s_scr[slot, c, j * kc:(j + 1) * kc, :] = s
            pm = part(s, jnp.maximum)
            m[c] = pm if m[c] is None else jnp.maximum(m[c], pm)

    def values(slot, j, m, l, acc):
        vt = vt_scr[:, j * kc:(j + 1) * kc]
        for c in range(2):
            p = jnp.exp2(s_scr[slot, c, j * kc:(j + 1) * kc, :] - m[c])
            ps = part(p, jnp.add)
            pv = jnp.dot(vt, p.astype(BF16), preferred_element_type=F32)
            l[c] = ps if l[c] is None else l[c] + ps
            acc[c] = pv if acc[c] is None else acc[c] + pv

    def finish(t, l, acc):
        l = [jnp.sum(x, axis=0, keepdims=True) for x in l]
        o = acc[0] * (1.0 / l[0]) - lam * (acc[1] * (1.0 / l[1]))
        ms = jnp.mean(o * o, axis=0, keepdims=True)
        y = (o * lax.rsqrt(ms + EPS) * subln_ref[...]) * (1.0 - lam_init)
        o_ref[tile_rows(t), :] = y.T

    def phase(t, slot, m_prev, do_scores, do_values):
        qz = load_q(t) if do_scores else None
        m, l, acc = [None, None], [None, None], [None, None]
        for j in range(n_chunks):
            if do_scores:
                scores(slot, j, qz, m)
            if do_values:
                values(1 - slot, j, m_prev, l, acc)
        if do_values:
            finish(t - 1, l, acc)
        if do_scores:
            return tuple(jnp.max(x, axis=0, keepdims=True) for x in m)
        return m_prev

    def phase_pair(i, m_prev):
        t = 2 * i + 1
        return phase(t + 1, 0, phase(t, 1, m_prev, True, True), True, True)

    m_prev = phase(0, 0, None, True, False)
    n_pairs = (n_tiles - 1) // 2
    if n_pairs:
        m_prev = lax.fori_loop(0, n_pairs, phase_pair, m_prev)
    if (n_tiles - 1) % 2:
        m_prev = phase(n_tiles - 1, 1, m_prev, True, True)
    phase(n_tiles, n_tiles % 2, m_prev, False, True)


def _diff_attn_cached_kernel(q_ref, k_ref, v_ref, ck_ref, cv_ref, cos_ref, sin_ref, lam_ref, subln_ref,
                             o_ref, k_scr, vt_scr, s_scr, **kw):
    _diff_attn_body(q_ref, k_ref, v_ref, ck_ref, cv_ref, cos_ref, sin_ref, lam_ref, subln_ref,
                    o_ref, k_scr, vt_scr, s_scr, **kw)


def _diff_attn_context_kernel(q_ref, k_ref, v_ref, lam_ref, subln_ref, o_ref, k_scr, vt_scr, s_scr, **kw):
    _diff_attn_body(q_ref, k_ref, v_ref, None, None, None, None, lam_ref, subln_ref,
                    o_ref, k_scr, vt_scr, s_scr, **kw)


def _diff_attention(qkv, lam_vecs, subln, lam_init, nb, seq_len, cache=None, rope_tables=None):
    tq = ATT_QUERY_TILE
    n_tiles = min(ATT_TILES_PER_STEP, seq_len // tq)
    tqs = n_tiles * tq
    nq = seq_len // tqs
    width = 2 * DIFF_DH
    n_cache = 0 if cache is None else cache[0].shape[1]
    n_keys = n_cache + seq_len
    in_specs = [pl.BlockSpec((tqs, width), lambda b, h, i: (b * nq + i, h)),
                pl.BlockSpec((seq_len, width), lambda b, h, i: (b, DIFF_HEADS + h)),
                pl.BlockSpec((seq_len, width), lambda b, h, i: (b, 2 * DIFF_HEADS + h))]
    args = [qkv, qkv, qkv]
    if cache is not None:
        in_specs += [pl.BlockSpec((1, n_cache, width), lambda b, h, i: (b, 0, h))] * 2
        in_specs += [pl.BlockSpec((seq_len, width), lambda b, h, i: (0, 0))] * 2
        args += [cache[0], cache[1], rope_tables[0], rope_tables[1]]
        kern = _diff_attn_cached_kernel
    else:
        kern = _diff_attn_context_kernel
    in_specs += [pl.BlockSpec((4, DIFF_DH), lambda b, h, i: (0, 0)),
                 pl.BlockSpec((width, 1), lambda b, h, i: (0, 0))]
    args += [lam_vecs, subln.reshape(width, 1)]
    return pl.pallas_call(
        functools.partial(kern, n_cache=n_cache, tq=tq, lam_init=lam_init),
        grid=(nb, DIFF_HEADS, nq),
        in_specs=in_specs,
        out_specs=pl.BlockSpec((tqs, width), lambda b, h, i: (b * nq + i, h)),
        out_shape=jax.ShapeDtypeStruct((nb * seq_len, D_MODEL), F32),
        scratch_shapes=[pltpu.VMEM((n_keys, width), BF16), pltpu.VMEM((width, n_keys), BF16),
                        pltpu.VMEM((min(n_tiles, 2), 2, n_keys, tq), F32)],
        compiler_params=_cparams("parallel", "parallel", "arbitrary"),
        name="diff_attention",
    )(*args)


def _rope_tables(n_tokens):
    rows = (jnp.arange(n_tokens, dtype=jnp.int32) // GRID_W).astype(F32)
    cols = (jnp.arange(n_tokens, dtype=jnp.int32) % GRID_W).astype(F32)
    n_freq = DIFF_DH // 4
    inv_freq = ROPE_BASE ** (-jnp.arange(n_freq, dtype=F32) / n_freq)
    ang_r = rows[:, None] * inv_freq
    ang_c = cols[:, None] * inv_freq
    ang = jnp.concatenate([ang_r, ang_r, ang_c, ang_c] * 2, axis=-1)
    sign = jnp.where((jnp.arange(2 * DIFF_DH) % (2 * ROPE_HALF)) < ROPE_HALF, -1.0, 1.0).astype(F32)
    return jnp.cos(ang), jnp.sin(ang) * sign


GDN_QKV = 3 * GDN_HEADS * GDN_DK
GDN_PROJ = GDN_QKV + D_MODEL + LANES
GDN_SEQS_PER_STEP = 2


def _cumsum_rows(x, reverse):
    n = x.shape[0]
    row = lax.broadcasted_iota(jnp.int32, x.shape, 0)
    k = 1
    while k < n:
        if reverse:
            x = x + jnp.where(row < n - k, pltpu.roll(x, n - k, 0), 0.0)
        else:
            x = x + jnp.where(row >= k, pltpu.roll(x, k, 0), 0.0)
        k *= 2
    return x


def _gdn_units(units, eye, blocks):
    c = GDN_CHUNK
    for n in units:
        n["decay"] = jnp.where(n["incl"], jnp.exp(jnp.where(n["incl"], n["gcol"] - n["grow"], 0.0)), 0.0)
        n["kb"] = n["k"] * n["bcol"]
        n["kt"] = n["k"].T
    for n in units:
        a = _dot(jnp.concatenate([n["kb"], n["q"]], axis=0), n["kt"])
        n["lower"] = jnp.where(n["strict"], a[:c] * n["decay"], 0.0)
        n["intra"] = a[c:] * n["decay"]
        n["t_off"] = -jnp.where(blocks[0], n["lower"], 0.0)
    for blk in blocks[1:]:
        for n in units:
            n["p"] = _dot(eye + n["t_off"], jnp.where(blk, n["lower"], 0.0))
        for n in units:
            n["t_off"] = n["t_off"] - _dot(n["p"], eye + n["t_off"])
    for n in units:
        egc = jnp.exp(n["gcol"])
        rhs = jnp.concatenate([n["v"] * n["bcol"], n["kb"] * egc], axis=1)
        sol = rhs + _dot(n["t_off"], rhs)
        n["u"] = sol[:, :GDN_DV]
        n["wq"] = jnp.concatenate([sol[:, GDN_DV:], n["q"] * egc], axis=0)
    for n in units:
        ws = _dot(n["wq"], n["s"])
        n["v_new"] = n["u"] - ws[:c]
        n["qs"] = ws[c:]
        n["ikt"] = jnp.concatenate([n["intra"], n["kt"] * jnp.exp(n["gtot"] - n["grow"])], axis=0)
    outs = []
    for n in units:
        r = _dot(n["ikt"], n["v_new"])
        outs.append((n["qs"] + r[:c], n["s"] * jnp.exp(n["gtot"]) + r[c:]))
    return outs


def _gdn_prep_kernel(m_ref, p_ref, n_ref, ab_ref, cw_ref, alog_ref, dtb_ref, qkv_ref, gate_ref, ext_scr,
                     *, tm, seq_len):
    i = pl.program_id(0)
    keep_prev = jnp.where((i * tm) % seq_len == 0, 0.0, 1.0)
    keep_next = jnp.where(((i + 1) * tm) % seq_len == 0, 0.0, 1.0)
    ext_scr[0:SUBLANES] = p_ref[...] * keep_prev
    ext_scr[SUBLANES:SUBLANES + tm] = m_ref[...]
    ext_scr[SUBLANES + tm:] = n_ref[...] * keep_next
    ones = jnp.ones((GDN_DK, GDN_DK), BF16)
    for j in range(3 * GDN_HEADS):
        cols = slice(j * GDN_DK, (j + 1) * GDN_DK)
        cw = cw_ref[:, cols]
        x = _silu(cw[0:1] * ext_scr[SUBLANES - 1:SUBLANES - 1 + tm, cols]
                  + cw[1:2] * ext_scr[SUBLANES:SUBLANES + tm, cols]
                  + cw[2:3] * ext_scr[SUBLANES + 1:SUBLANES + 1 + tm, cols])
        if j < 2 * GDN_HEADS:
            sq = x * x
            hi = sq.astype(BF16)
            lo = (sq - hi.astype(F32)).astype(BF16)
            ssq = jnp.dot(hi, ones, preferred_element_type=F32) + jnp.dot(lo, ones, preferred_element_type=F32)
            x = x * lax.rsqrt(ssq + EPS)
            if j < GDN_HEADS:
                x = x * (GDN_DK ** -0.5)
        qkv_ref[:, cols] = x
    ab = ab_ref[...]
    z = ab + dtb_ref[...]
    g = -jnp.exp(alog_ref[...]) * (jnp.maximum(z, 0.0) + jnp.log(1.0 + jnp.exp(-jnp.abs(z))))
    lane = lax.broadcasted_iota(jnp.int32, ab.shape, 1)
    gate_ref[...] = jnp.where(lane < 2 * GDN_HEADS, g, jax.nn.sigmoid(ab))


def _gdn_prep(proj, conv_w, a_log, dt_bias, seq_len):
    t = proj.shape[0]
    tm = min(FFN_ROW_TILE, seq_len)
    r8 = tm // SUBLANES
    nblk8 = t // SUBLANES
    ab_col = (GDN_QKV + D_MODEL) // LANES
    pad = lambda p: jnp.pad(p.reshape(1, 2 * GDN_HEADS), ((0, 0), (0, LANES - 2 * GDN_HEADS)))
    return pl.pallas_call(
        functools.partial(_gdn_prep_kernel, tm=tm, seq_len=seq_len),
        grid=(t // tm,),
        in_specs=[pl.BlockSpec((tm, GDN_QKV), lambda i: (i, 0)),
                  pl.BlockSpec((SUBLANES, GDN_QKV), lambda i: (jnp.maximum(i * r8 - 1, 0), 0)),
                  pl.BlockSpec((SUBLANES, GDN_QKV), lambda i: (jnp.minimum((i + 1) * r8, nblk8 - 1), 0)),
                  pl.BlockSpec((tm, LANES), lambda i: (i, ab_col)),
                  _const_spec((3, GDN_QKV)), _const_spec((1, LANES)), _const_spec((1, LANES))],
        out_specs=[pl.BlockSpec((tm, GDN_QKV), lambda i: (i, 0)), pl.BlockSpec((tm, LANES), lambda i: (i, 0))],
        out_shape=[jax.ShapeDtypeStruct((t, GDN_QKV), F32), jax.ShapeDtypeStruct((t, LANES), F32)],
        scratch_shapes=[pltpu.VMEM((tm + 2 * SUBLANES, GDN_QKV), F32)],
        compiler_params=_cparams("parallel"),
        name="gdn_prep",
    )(proj, proj, proj, proj, conv_w, pad(a_log), pad(dt_bias))


def _gdn_kernel(fq_ref, fg_ref, bq_ref, bg_ref, *rest, has_s0):
    if has_s0:
        s0_ref, of_ref, ob_ref, s_ref = rest
    else:
        of_ref, ob_ref, s_ref = rest
    c = pl.program_id(1)

    @pl.when(c == 0)
    def _():
        if has_s0:
            s_ref[...] = s0_ref[...]
        else:
            s_ref[...] = jnp.zeros_like(s_ref)

    ri = lax.broadcasted_iota(jnp.int32, (GDN_CHUNK, GDN_CHUNK), 0)
    ci = lax.broadcasted_iota(jnp.int32, (GDN_CHUNK, GDN_CHUNK), 1)
    eye = jnp.where(ri == ci, 1.0, 0.0)
    blocks = []
    size = 1
    while size < GDN_CHUNK:
        blocks.append((ri // (2 * size) == ci // (2 * size)) & (ri // size != ci // size))
        size *= 2

    units = []
    for b in range(GDN_SEQS_PER_STEP):
        for d, (q_ref, g_ref) in enumerate(((fq_ref, fg_ref), (bq_ref, bg_ref))):
            gates = g_ref[b]
            gc = _cumsum_rows(gates, reverse=(d == 1))
            gct = jnp.concatenate([gc, gc], axis=0).T
            if d == 0:
                incl, strict, last = ri >= ci, ri > ci, GDN_CHUNK - 1
            else:
                incl, strict, last = ri <= ci, ri < ci, 0
            for h in range(GDN_HEADS):
                col = d * GDN_HEADS + h
                units.append(dict(
                    q=q_ref[b, :, h * GDN_DK:(h + 1) * GDN_DK],
                    k=q_ref[b, :, (GDN_HEADS + h) * GDN_DK:(GDN_HEADS + h + 1) * GDN_DK],
                    v=q_ref[b, :, (2 * GDN_HEADS + h) * GDN_DK:(2 * GDN_HEADS + h + 1) * GDN_DK],
                    gcol=gc[:, col:col + 1], grow=gct[col:col + 1, 0:GDN_CHUNK],
                    bcol=gates[:, 2 * GDN_HEADS + col:2 * GDN_HEADS + col + 1],
                    gtot=gc[last:last + 1, col:col + 1], s=s_ref[b, d, h], incl=incl, strict=strict))

    outs = iter(_gdn_units(units, eye, blocks))
    for b in range(GDN_SEQS_PER_STEP):
        for d, o_ref in enumerate((of_ref, ob_ref)):
            for h in range(GDN_HEADS):
                o, s_new = next(outs)
                o_ref[b, :, h * GDN_DV:(h + 1) * GDN_DV] = o
                s_ref[b, d, h] = s_new


def _gdn_scan(qkv, gates, nb, seq_len, s0=None):
    n = seq_len // GDN_CHUNK
    g = GDN_SEQS_PER_STEP
    hv = GDN_HEADS * GDN_DV

    def chunk_specs(chunk_of):
        return [pl.BlockSpec((g, GDN_CHUNK, GDN_QKV), lambda b, c: (b, chunk_of(c), 0)),
                pl.BlockSpec((g, GDN_CHUNK, LANES), lambda b, c: (b, chunk_of(c), 0))]

    state_spec = pl.BlockSpec((g, 2, GDN_HEADS, GDN_DK, GDN_DV), lambda b, c: (b, 0, 0, 0, 0))
    in_specs = chunk_specs(lambda c: c) + chunk_specs(lambda c: n - 1 - c)
    qkv = qkv.reshape(nb, seq_len, GDN_QKV)
    gates = gates.reshape(nb, seq_len, LANES)
    args = [qkv, gates, qkv, gates]
    if s0 is not None:
        in_specs.append(state_spec)
        args.append(s0)
    o_f, o_b, s_fin = pl.pallas_call(
        functools.partial(_gdn_kernel, has_s0=s0 is not None),
        grid=(nb // g, n),
        in_specs=in_specs,
        out_specs=[pl.BlockSpec((g, GDN_CHUNK, hv), lambda b, c: (b, c, 0)),
                   pl.BlockSpec((g, GDN_CHUNK, hv), lambda b, c: (b, n - 1 - c, 0)),
                   state_spec],
        out_shape=[jax.ShapeDtypeStruct((nb, seq_len, hv), F32), jax.ShapeDtypeStruct((nb, seq_len, hv), F32),
                   jax.ShapeDtypeStruct((nb, 2, GDN_HEADS, GDN_DK, GDN_DV), F32)],
        compiler_params=_cparams("parallel", "arbitrary"),
        name="gdn_scan",
    )(*args)
    return o_f.reshape(nb * seq_len, hv), o_b.reshape(nb * seq_len, hv), s_fin


def _gdn_mixer(x, mod, gains, w_cat, conv_w, a_log, dt_bias, head_gain, w_out, nb, seq_len, rows_per_mod,
               s0=None):
    proj = _norm_linear(x, mod, gains[0][None], w_cat, rows_per_mod, LANES * 11)
    qkv, gates = _gdn_prep(proj, conv_w, a_log, dt_bias, seq_len)
    o_f, o_b, s_fin = _gdn_scan(qkv, gates, nb, seq_len, s0)
    x = _gdn_out(x, o_f, o_b, proj, head_gain[None], mod, gains[1][None], w_out, rows_per_mod)
    return x, s_fin


def _diff_mixer(x, mod, gains, w_qkv, lam_vecs, subln, w_out, lam_init, nb, seq_len, rows_per_mod,
                cache=None, rope_tables=None):
    qkv = _norm_linear(x, mod, gains[0][None], w_qkv, rows_per_mod, D_MODEL)
    heads = _diff_attention(qkv, lam_vecs, subln, lam_init, nb, seq_len, cache, rope_tables)
    x = _plain_out(x, heads, mod, gains[1][None], w_out, rows_per_mod)
    return x, qkv


S5_IN = S5_CHUNK * S5_GROUP_CH
S5_BITS = 5


def _cmul(ar, ai, br, bi):
    return ar * br - ai * bi, ar * bi + ai * br


def _discretize(a_re, a_im, log_dt):
    dt = jnp.exp(log_dt)
    mag = jnp.exp(dt * a_re)
    abar_re = mag * jnp.cos(dt * a_im)
    abar_im = mag * jnp.sin(dt * a_im)
    den = a_re * a_re + a_im * a_im
    f_re = ((abar_re - 1.0) * a_re + abar_im * a_im) / den
    f_im = (abar_im * a_re - (abar_re - 1.0) * a_im) / den
    return abar_re, abar_im, f_re, f_im


def _cpow(ar, ai, expo):
    pr = jnp.ones(expo.shape, F32)
    pi = jnp.zeros(expo.shape, F32)
    for b in range(S5_BITS):
        bit = ((expo >> b) & 1) == 1
        pr, pi = _cmul(pr, pi, jnp.where(bit, ar, 1.0), jnp.where(bit, ai, 0.0))
        if b + 1 < S5_BITS:
            ar, ai = _cmul(ar, ai, ar, ai)
    return pr, pi


def _s5_ops_kernel(arow_ref, acol_ref, bpc_ref, bt_ref, ccat_ref, csep_ref, mc_ref, ef_ref, a16_ref):
    arow = arow_ref[0]
    ar, ai, fr, fi = _discretize(arow[0:1], arow[1:2], arow[2:3])
    row = lax.broadcasted_iota(jnp.int32, (S5_IN, LANES), 0)
    lane = lax.broadcasted_iota(jnp.int32, (S5_IN, LANES), 1)
    tok = row // S5_GROUP_CH
    fwd = lane < S5_STATE
    tile = lambda x: jnp.concatenate([x] * S5_CHUNK, axis=0)

    pr, pi = _cpow(ar, ai, jnp.where(fwd, S5_CHUNK - 1 - tok, tok))
    bt = bt_ref[0]
    bbr, bbi = _cmul(fr, fi, bt[0], bt[1])
    er, ei = _cmul(pr, pi, tile(bbr), tile(bbi))
    ef_ref[0, 0] = er
    ef_ref[0, 1] = ei
    pr, pi = _cpow(ar, ai, jnp.where(fwd, tok + 1, S5_CHUNK - tok))
    cc = ccat_ref[0]
    gr, gi = _cmul(pr, pi, tile(cc[0]), tile(cc[1]))
    ef_ref[0, 2] = gr
    ef_ref[0, 3] = -gi
    p16r, p16i = _cpow(ar, ai, jnp.full((1, LANES), S5_CHUNK, jnp.int32))
    a16_ref[0, 0:1] = p16r
    a16_ref[0, 1:2] = p16i

    acol = acol_ref[0]
    lane_w = lax.broadcasted_iota(jnp.int32, (S5_GROUP_CH, S5_IN), 1)
    place = jnp.where(lane_w % S5_GROUP_CH == lax.broadcasted_iota(jnp.int32, (S5_GROUP_CH, S5_IN), 0), 1.0, 0.0)
    lag = lax.broadcasted_iota(jnp.int32, (S5_STATE, S5_IN), 1) // S5_GROUP_CH
    kw = []
    for d in range(2):
        cr, ci, gr_, gi_ = _discretize(acol[:, 3 * d:3 * d + 1], acol[:, 3 * d + 1:3 * d + 2],
                                       acol[:, 3 * d + 2:3 * d + 3])
        bbr, bbi = _cmul(gr_, gi_, bpc_ref[0, 2 * d], bpc_ref[0, 2 * d + 1])
        pr, pi = _cpow(jnp.broadcast_to(cr, lag.shape), jnp.broadcast_to(ci, lag.shape),
                       S5_CHUNK - 1 - lag if d == 0 else lag)
        hr, hi = _cmul(pr, pi, _dot_f32(bbr, place), _dot_f32(bbi, place))
        kw.append(_dot_f32(csep_ref[0, 2 * d], hr) - _dot_f32(csep_ref[0, 2 * d + 1], hi))
    for t in range(S5_CHUNK):
        sh = (S5_CHUNK - 1 - t) * S5_GROUP_CH
        m_f = jnp.where(lane_w < (t + 1) * S5_GROUP_CH, pltpu.roll(kw[0], (S5_IN - sh) % S5_IN, 1), 0.0)
        m_b = jnp.where(lane_w >= t * S5_GROUP_CH, pltpu.roll(kw[1], t * S5_GROUP_CH, 1), 0.0)
        mc_ref[0, t * S5_GROUP_CH:(t + 1) * S5_GROUP_CH, :] = m_f + m_b


def _s5_operators(a_re, a_im, log_dt, b_re, b_im, c_re, c_im):
    g = S5_GROUPS
    cat = lambda x: jnp.concatenate([x[0], x[1]], axis=-1)
    dt_row = jnp.broadcast_to(log_dt[:, :, None], (2, g, S5_STATE))
    arow = jnp.stack([cat(a_re), cat(a_im), cat(dt_row)], axis=1)
    acol = jnp.stack([a_re[0], a_im[0], dt_row[0], a_re[1], a_im[1], dt_row[1]], axis=-1)
    bpc = jnp.stack([b_re[0], b_im[0], b_re[1], b_im[1]], axis=1)
    swap = lambda x: jnp.swapaxes(x, -1, -2)
    bt = jnp.stack([cat(swap(b_re)), cat(swap(b_im))], axis=1)
    ccat = jnp.stack([cat(c_re), cat(c_im)], axis=1)
    csep = jnp.stack([c_re[0], c_im[0], c_re[1], c_im[1]], axis=1)
    spec = lambda *s: pl.BlockSpec((1,) + s, lambda i: (i,) + (0,) * len(s))
    return pl.pallas_call(
        _s5_ops_kernel,
        grid=(g,),
        in_specs=[spec(3, LANES), spec(S5_STATE, 6), spec(4, S5_STATE, S5_GROUP_CH),
                  spec(2, S5_GROUP_CH, LANES), spec(2, S5_GROUP_CH, LANES), spec(4, S5_GROUP_CH, S5_STATE)],
        out_specs=[spec(S5_IN, S5_IN), spec(4, S5_IN, LANES), spec(2, LANES)],
        out_shape=[jax.ShapeDtypeStruct((g, S5_IN, S5_IN), F32),
                   jax.ShapeDtypeStruct((g, 4, S5_IN, LANES), F32),
                   jax.ShapeDtypeStruct((g, 2, LANES), F32)],
        compiler_params=_cparams("parallel"),
        name="s5_operators",
    )(arow, acol, bpc, bt, ccat, csep)


def _s5_scan_kernel(u_ref, mc_ref, ef_ref, a16_ref, *rest, nb, has_x0):
    if has_x0:
        x0r_ref, x0i_ref, y_ref, fr_ref, fi_ref, xr_scr, xi_scr = rest
    else:
        y_ref, fr_ref, fi_ref, xr_scr, xi_scr = rest
    u = u_ref[0].astype(BF16)
    rows = u.shape[0]
    nc = rows // nb
    vr = jnp.dot(u, ef_ref[0, 0].astype(BF16), preferred_element_type=F32)
    vi = jnp.dot(u, ef_ref[0, 1].astype(BF16), preferred_element_type=F32)
    row_c = lax.broadcasted_iota(jnp.int32, (rows, LANES), 0) % nc
    fwd = lax.broadcasted_iota(jnp.int32, (rows, LANES), 1) < S5_STATE
    pr = a16_ref[0, 0:1]
    pi = a16_ref[0, 1:2]
    if has_x0:
        expand = lambda x: jnp.broadcast_to(x[:, None, :], (nb, nc, LANES)).reshape(rows, LANES)
        x0r, x0i = expand(x0r_ref[0]), expand(x0i_ref[0])
        first = row_c == jnp.where(fwd, 0, nc - 1)
        ir, ii = _cmul(pr, pi, x0r, x0i)
        vr = vr + jnp.where(first, ir, 0.0)
        vi = vi + jnp.where(first, ii, 0.0)

    def shifted(x, k):
        down = jnp.where(row_c >= k, pltpu.roll(x, k, 0), 0.0)
        up = jnp.where(row_c < nc - k, pltpu.roll(x, rows - k, 0), 0.0)
        return jnp.where(fwd, down, up)

    k = 1
    while k < nc:
        sr, si = _cmul(pr, pi, shifted(vr, k), shifted(vi, k))
        vr, vi = vr + sr, vi + si
        pr, pi = _cmul(pr, pi, pr, pi)
        k *= 2
    er, ei = shifted(vr, 1), shifted(vi, 1)
    if has_x0:
        er = jnp.where(first, x0r, er)
        ei = jnp.where(first, x0i, ei)
    y = lax.dot_general(u, mc_ref[0].astype(BF16), (((1,), (1,)), ((), ())), preferred_element_type=F32)
    y = y + _dot_nt(er, ef_ref[0, 2]) + _dot_nt(ei, ef_ref[0, 3])
    y_ref[0] = y
    xr_scr[...] = vr
    xi_scr[...] = vi
    lane_b = lax.broadcasted_iota(jnp.int32, (nb, LANES), 1) < S5_STATE
    fr_ref[0] = jnp.where(lane_b, xr_scr[pl.ds(nc - 1, nb, stride=nc), :], xr_scr[pl.ds(0, nb, stride=nc), :])
    fi_ref[0] = jnp.where(lane_b, xi_scr[pl.ds(nc - 1, nb, stride=nc), :], xi_scr[pl.ds(0, nb, stride=nc), :])


def _s5_scan(u, ops, nb, x0=None):
    mc, ef, a16 = ops
    g, rows, _ = u.shape
    spec = lambda *s: pl.BlockSpec((1,) + s, lambda i: (i,) + (0,) * len(s))
    in_specs = [spec(rows, S5_IN), spec(S5_IN, S5_IN), spec(4, S5_IN, LANES), spec(2, LANES)]
    args = [u, mc, ef, a16]
    if x0 is not None:
        in_specs += [spec(nb, LANES), spec(nb, LANES)]
        args += list(x0)
    return pl.pallas_call(
        functools.partial(_s5_scan_kernel, nb=nb, has_x0=x0 is not None),
        grid=(g,),
        in_specs=in_specs,
        out_specs=[spec(rows, S5_IN), spec(nb, LANES), spec(nb, LANES)],
        out_shape=[jax.ShapeDtypeStruct((g, rows, S5_IN), F32),
                   jax.ShapeDtypeStruct((g, nb, LANES), F32),
                   jax.ShapeDtypeStruct((g, nb, LANES), F32)],
        scratch_shapes=[pltpu.VMEM((rows, LANES), F32), pltpu.VMEM((rows, LANES), F32)],
        compiler_params=_cparams("parallel"),
        name="s5_scan",
    )(*args)


def _to_chunk_major(h, nb, seq_len):
    nc = seq_len // S5_CHUNK
    h = h.reshape(nb * nc, S5_CHUNK, S5_GROUPS, S5_GROUP_CH)
    return jnp.transpose(h, (2, 0, 1, 3)).reshape(S5_GROUPS, nb * nc, S5_IN)


def _from_chunk_major(y, nb, seq_len):
    nc = seq_len // S5_CHUNK
    y = y.reshape(S5_GROUPS, nb * nc, S5_CHUNK, S5_GROUP_CH)
    return jnp.transpose(y, (1, 2, 0, 3)).reshape(nb * seq_len, D_MODEL)


def _s5_mixer(x, mod, gains, ops, d_skip, w_glu, nb, seq_len, rows_per_mod, cache=None):
    h = _norm_mod(x, mod, gains[0][None], rows_per_mod)
    x0 = None
    if cache is not None:
        x0 = [jnp.transpose(s, (2, 0, 1, 3)).reshape(S5_GROUPS, nb, 2 * S5_STATE) for s in cache]
    y, fin_re, fin_im = _s5_scan(_to_chunk_major(h, nb, seq_len), ops, nb, x0)
    x = _s5_out(x, h, _from_chunk_major(y, nb, seq_len), d_skip[None], mod, gains[1][None], w_glu, rows_per_mod)
    unpack = lambda f: jnp.transpose(f.reshape(S5_GROUPS, nb, 2, S5_STATE), (1, 2, 0, 3))
    return x, (unpack(fin_re), unpack(fin_im))


def kernel(x_prompt, x_sample, state_l0_gdn, cache_l1_k, cache_l1_v, state_l2_s5_re, state_l2_s5_im, state_l3_gdn, c, c_ctx, w_mod, b_mod, norm_gain, w_ffn_up, ffn_conv, w_ffn_down, w_gdn_qkv, gdn_conv, w_gdn_gate, w_gdn_alpha, w_gdn_beta, gdn_a_log, gdn_dt_bias, gdn_norm, w_gdn_out, w_diff_qkv, diff_lam, diff_subln, w_diff_out, s5_a_re, s5_a_im, s5_log_dt, s5_b_re, s5_b_im, s5_c_re, s5_c_im, s5_d, w_s5_glu):
    nbp, lp = x_prompt.shape[:2]
    nbs, ls = x_sample.shape[:2]
    tp, ts = nbp * lp, nbs * ls
    pad_rows = -(nbs + 1) % SUBLANES
    cond = jnp.concatenate([c, c_ctx[None], jnp.zeros((pad_rows, D_MODEL), F32)], axis=0)
    mod_all = _modulation(cond, w_mod, b_mod)
    s5_ops = _s5_operators(s5_a_re[0], s5_a_im[0], s5_log_dt[0], s5_b_re[0], s5_b_im[0], s5_c_re[0], s5_c_im[0])
    rope = _rope_tables(ls)
    past = cache_l1_k.shape[1]
    attn_cache = (cache_l1_k.reshape(nbs, past, D_MODEL), cache_l1_v.reshape(nbs, past, D_MODEL))
    gdn_caches = (state_l0_gdn, state_l3_gdn)
    bf = lambda w: w.astype(BF16)

    xp = x_prompt.reshape(tp, D_MODEL)
    xs = x_sample.reshape(ts, D_MODEL)
    ctx_states = []
    for l in range(DEPTH):
        kind, j = l % 3, l // 3
        gains = norm_gain[l]
        mod_s = mod_all[l, :nbs][:, None, :]
        mod_p = mod_all[l, nbs:nbs + 1][:, None, :]
        if kind == 0:
            w_cat = jnp.concatenate([w_gdn_qkv[j], w_gdn_gate[j], w_gdn_alpha[j], w_gdn_beta[j],
                                     jnp.zeros((D_MODEL, LANES - 4 * GDN_HEADS), F32)], axis=1).astype(BF16)
            gdn = functools.partial(_gdn_mixer, gains=gains, w_cat=w_cat, conv_w=gdn_conv[j], a_log=gdn_a_log[j],
                                    dt_bias=gdn_dt_bias[j], head_gain=gdn_norm[j], w_out=bf(w_gdn_out[j]))
            xp, st = gdn(xp, mod_p, nb=nbp, seq_len=lp, rows_per_mod=tp)
            xs, _ = gdn(xs, mod_s, nb=nbs, seq_len=ls, rows_per_mod=ls, s0=gdn_caches[j])
            ctx_states.append(st)
        elif kind == 1:
            lam_init = 0.8 - 0.6 * math.exp(-0.3 * l)
            attn = functools.partial(_diff_mixer, gains=gains, w_qkv=bf(w_diff_qkv[j]), lam_vecs=diff_lam[j],
                                     subln=diff_subln[j], w_out=bf(w_diff_out[j]), lam_init=lam_init)
            xp, qkv = attn(xp, mod_p, nb=nbp, seq_len=lp, rows_per_mod=tp)
            xs, _ = attn(xs, mod_s, nb=nbs, seq_len=ls, rows_per_mod=ls, cache=attn_cache, rope_tables=rope)
            ctx_states.append((qkv[:, D_MODEL:2 * D_MODEL].reshape(nbp, lp, DIFF_HEADS, 2, DIFF_DH),
                               qkv[:, 2 * D_MODEL:].reshape(nbp, lp, DIFF_HEADS, 2 * DIFF_DH)))
        else:
            s5 = functools.partial(_s5_mixer, gains=gains, ops=s5_ops, d_skip=s5_d[j], w_glu=bf(w_s5_glu[j]))
            xp, st = s5(xp, mod_p, nb=nbp, seq_len=lp, rows_per_mod=tp)
            xs, _ = s5(xs, mod_s, nb=nbs, seq_len=ls, rows_per_mod=ls, cache=(state_l2_s5_re, state_l2_s5_im))
            ctx_states.append(st)
        ffn = functools.partial(_conv_ffn, gain2=gains[2][None], gain3=gains[3][None], w_up=bf(w_ffn_up[l]),
                                conv_w=ffn_conv[l], w_down=bf(w_ffn_down[l]))
        xp = ffn(xp, mod_p, seq_len=lp, rows_per_mod=tp)
        xs = ffn(xs, mod_s, seq_len=ls, rows_per_mod=ls)
    st0, (k1, v1), (s2_re, s2_im), st3 = ctx_states
    return (xp.reshape(nbp, lp, D_MODEL), xs.reshape(nbs, ls, D_MODEL), st0, k1, v1, s2_re, s2_im, st3)
```

```python
import functools
import math

import jax
import jax.numpy as jnp
from jax import lax
from jax.experimental import pallas as pl
from jax.experimental.pallas import tpu as pltpu

F32 = jnp.float32
BF16 = jnp.bfloat16

D_MODEL = 1024
DEPTH = 4
GRID_W = 64
GDN_HEADS = 8
GDN_DK = 128
GDN_DV = 128
GDN_CHUNK = 64
DIFF_DH = 64
DIFF_HEADS = 8
ROPE_BASE = 10000.0
S5_GROUP_CH = 16
S5_GROUPS = 64
S5_STATE = 64
FFN_DIM = 2816
MOD_CHUNKS = 6
EPS = 1e-6

SUBLANES = 8
LANES = 128
VMEM_LIMIT_BYTES = 56 * 1024 * 1024

ROW_TILE = 1024
FFN_ROW_TILE = 512
FFN_CHUNK = 256
S5_CHUNK = 16
ATT_KEY_CHUNK = 256
ATT_QUERY_TILE = 256
ATT_TILES_PER_STEP = 16


def _cparams(*sem):
    return pltpu.CompilerParams(dimension_semantics=sem, vmem_limit_bytes=VMEM_LIMIT_BYTES)


def _silu(x):
    return x * jax.nn.sigmoid(x)


def _rms(x, gain):
    return x * lax.rsqrt(jnp.mean(x * x, axis=-1, keepdims=True) + EPS) * gain


def _mod_slice(mod_ref, idx):
    return mod_ref[0, :, idx * D_MODEL:(idx + 1) * D_MODEL]


def _rms_mod(x, gain, mod_ref, shift_idx, scale_idx):
    return _rms(x, gain) * (1.0 + _mod_slice(mod_ref, scale_idx)) + _mod_slice(mod_ref, shift_idx)


def _residual(x, y, gain, mod_ref, gate_idx):
    return x + _mod_slice(mod_ref, gate_idx) * _rms(y, gain)


def _dot(a, b):
    return jnp.dot(a.astype(BF16), b.astype(BF16), preferred_element_type=F32)


def _dot_nt(a, b):
    return lax.dot_general(a.astype(BF16), b.astype(BF16), (((1,), (1,)), ((), ())),
                           preferred_element_type=F32)


def _dot_tn(a, b):
    return lax.dot_general(a.astype(BF16), b.astype(BF16), (((0,), (0,)), ((), ())),
                           preferred_element_type=F32)


def _dot_f32(a, b):
    return jnp.dot(a, b, preferred_element_type=F32, precision=lax.Precision.HIGHEST)


def _mod_kernel(c_ref, w_ref, b_ref, o_ref):
    o_ref[0] = _dot(_silu(c_ref[...]), w_ref[0]) + b_ref[0]


def _modulation(cond, w_mod, b_mod):
    rows = cond.shape[0]
    n = w_mod.shape[-1]
    tn = 1536
    return pl.pallas_call(
        _mod_kernel,
        grid=(DEPTH, n // tn),
        in_specs=[pl.BlockSpec((rows, D_MODEL), lambda l, j: (0, 0)),
                  pl.BlockSpec((1, D_MODEL, tn), lambda l, j: (l, 0, j)),
                  pl.BlockSpec((1, 1, tn), lambda l, j: (l, 0, j))],
        out_specs=pl.BlockSpec((1, rows, tn), lambda l, j: (l, 0, j)),
        out_shape=jax.ShapeDtypeStruct((DEPTH, rows, n), F32),
        compiler_params=_cparams("parallel", "parallel"),
        name="modulation",
    )(cond, w_mod, b_mod.reshape(DEPTH, 1, n))


def _norm_linear_kernel(x_ref, mod_ref, g_ref, w_ref, o_ref, h_scr):
    @pl.when(pl.program_id(1) == 0)
    def _():
        h_scr[...] = _rms_mod(x_ref[...], g_ref[...], mod_ref, 0, 1).astype(BF16)

    o_ref[...] = jnp.dot(h_scr[...], w_ref[...].astype(BF16), preferred_element_type=F32)


def _mod_spec(tm, rows_per_mod):
    return pl.BlockSpec((1, 1, MOD_CHUNKS * D_MODEL), lambda i, *_: ((i * tm) // rows_per_mod, 0, 0))


def _norm_linear(x, mod, gain, w, rows_per_mod, tn):
    t = x.shape[0]
    n = w.shape[1]
    tm = min(ROW_TILE, rows_per_mod)
    return pl.pallas_call(
        _norm_linear_kernel,
        grid=(t // tm, n // tn),
        in_specs=[pl.BlockSpec((tm, D_MODEL), lambda i, j: (i, 0)),
                  _mod_spec(tm, rows_per_mod),
                  pl.BlockSpec((1, D_MODEL), lambda i, j: (0, 0)),
                  pl.BlockSpec((D_MODEL, tn), lambda i, j: (0, j))],
        out_specs=pl.BlockSpec((tm, tn), lambda i, j: (i, j)),
        out_shape=jax.ShapeDtypeStruct((t, n), F32),
        scratch_shapes=[pltpu.VMEM((tm, D_MODEL), BF16)],
        compiler_params=_cparams("parallel", "arbitrary"),
        name="norm_linear",
    )(x, mod, gain, w)


def _norm_mod_kernel(x_ref, mod_ref, g_ref, o_ref):
    o_ref[...] = _rms_mod(x_ref[...], g_ref[...], mod_ref, 0, 1)


def _norm_mod(x, mod, gain, rows_per_mod):
    t = x.shape[0]
    tm = min(ROW_TILE, rows_per_mod)
    return pl.pallas_call(
        _norm_mod_kernel,
        grid=(t // tm,),
        in_specs=[pl.BlockSpec((tm, D_MODEL), lambda i: (i, 0)),
                  _mod_spec(tm, rows_per_mod),
                  pl.BlockSpec((1, D_MODEL), lambda i: (0, 0))],
        out_specs=pl.BlockSpec((tm, D_MODEL), lambda i: (i, 0)),
        out_shape=jax.ShapeDtypeStruct((t, D_MODEL), F32),
        compiler_params=_cparams("parallel"),
        name="norm_mod",
    )(x, mod, gain)


def _plain_out_kernel(x_ref, y_ref, mod_ref, g_ref, w_ref, o_ref):
    y = _dot(y_ref[...], w_ref[...])
    o_ref[...] = _residual(x_ref[...], y, g_ref[...], mod_ref, 2)


def _gdn_out_kernel(x_ref, of_ref, ob_ref, gate_ref, hn_ref, mod_ref, g_ref, w_ref, o_ref, y_scr):
    o = of_ref[...] + ob_ref[...]
    gate = _silu(gate_ref[...])
    hn = hn_ref[...]
    for h in range(GDN_HEADS):
        sl = slice(h * GDN_DV, (h + 1) * GDN_DV)
        y_scr[:, sl] = (_rms(o[:, sl], hn) * gate[:, sl]).astype(BF16)
    y = jnp.dot(y_scr[...], w_ref[...].astype(BF16), preferred_element_type=F32)
    o_ref[...] = _residual(x_ref[...], y, g_ref[...], mod_ref, 2)


def _s5_out_kernel(x_ref, h_ref, ys_ref, d_ref, mod_ref, g_ref, w_ref, o_ref):
    y = jax.nn.gelu(d_ref[...] * h_ref[...] + ys_ref[...])
    z = _dot(y, w_ref[...])
    val = z[:, :D_MODEL]
    gate = z[:, D_MODEL:]
    o_ref[...] = _residual(x_ref[...], val * jax.nn.sigmoid(gate), g_ref[...], mod_ref, 2)


def _row_spec(tm, width=D_MODEL, col=0):
    return pl.BlockSpec((tm, width), lambda i: (i, col))


def _const_spec(shape):
    return pl.BlockSpec(shape, lambda i: (0,) * len(shape))


def _plain_out(x, y, mod, gain, w, rows_per_mod):
    t = x.shape[0]
    tm = min(ROW_TILE, rows_per_mod)
    return pl.pallas_call(
        _plain_out_kernel,
        grid=(t // tm,),
        in_specs=[_row_spec(tm), _row_spec(tm), _mod_spec(tm, rows_per_mod),
                  _const_spec((1, D_MODEL)), _const_spec((D_MODEL, D_MODEL))],
        out_specs=_row_spec(tm),
        out_shape=jax.ShapeDtypeStruct((t, D_MODEL), F32),
        compiler_params=_cparams("parallel"),
        name="plain_out",
    )(x, y, mod, gain, w)


def _gdn_out(x, o_f, o_b, proj, head_gain, mod, gain, w, rows_per_mod):
    t = x.shape[0]
    tm = min(ROW_TILE, rows_per_mod)
    return pl.pallas_call(
        _gdn_out_kernel,
        grid=(t // tm,),
        in_specs=[_row_spec(tm), _row_spec(tm), _row_spec(tm),
                  _row_spec(tm, D_MODEL, 3),
                  _const_spec((1, GDN_DV)), _mod_spec(tm, rows_per_mod),
                  _const_spec((1, D_MODEL)), _const_spec((D_MODEL, D_MODEL))],
        out_specs=_row_spec(tm),
        out_shape=jax.ShapeDtypeStruct((t, D_MODEL), F32),
        scratch_shapes=[pltpu.VMEM((tm, D_MODEL), BF16)],
        compiler_params=_cparams("parallel"),
        name="gdn_out",
    )(x, o_f, o_b, proj, head_gain, mod, gain, w)


def _s5_out(x, h, y_scan, d_skip, mod, gain, w, rows_per_mod):
    t = x.shape[0]
    tm = min(ROW_TILE // 2, rows_per_mod)
    return pl.pallas_call(
        _s5_out_kernel,
        grid=(t // tm,),
        in_specs=[_row_spec(tm), _row_spec(tm), _row_spec(tm), _const_spec((1, D_MODEL)),
                  _mod_spec(tm, rows_per_mod), _const_spec((1, D_MODEL)),
                  _const_spec((D_MODEL, 2 * D_MODEL))],
        out_specs=_row_spec(tm),
        out_shape=jax.ShapeDtypeStruct((t, D_MODEL), F32),
        compiler_params=_cparams("parallel"),
        name="s5_out",
    )(x, h, y_scan, d_skip, mod, gain, w)


def _ffn_kernel(x_ref, xp_ref, xn_ref, mod_ref, g2_ref, g3_ref, wu_ref, cw_ref, wd_ref, o_ref,
                h_scr, z_scr, act_scr, *, tm, seq_len):
    i = pl.program_id(0)
    ck = FFN_CHUNK
    g2 = g2_ref[...]
    keep_prev = jnp.where((i * tm) % seq_len == 0, 0.0, 1.0)
    keep_next = jnp.where(((i + 1) * tm) % seq_len == 0, 0.0, 1.0)
    h_scr[0:SUBLANES] = (_rms_mod(xp_ref[...], g2, mod_ref, 3, 4) * keep_prev).astype(BF16)
    h_scr[SUBLANES:SUBLANES + tm] = _rms_mod(x_ref[...], g2, mod_ref, 3, 4).astype(BF16)
    h_scr[SUBLANES + tm:] = (_rms_mod(xn_ref[...], g2, mod_ref, 3, 4) * keep_next).astype(BF16)
    h = h_scr[...]

    def conv(z_ref, cols, col0):
        cw = cw_ref[:, col0:col0 + ck]
        return (cw[0:1] * z_ref[SUBLANES - 1:SUBLANES - 1 + tm, cols]
                + cw[1:2] * z_ref[SUBLANES:SUBLANES + tm, cols]
                + cw[2:3] * z_ref[SUBLANES + 1:SUBLANES + 1 + tm, cols])

    for k in range(FFN_DIM // ck):
        z_ref = z_scr.at[k % 2]
        g0 = k * ck
        v0 = FFN_DIM + k * ck
        z_ref[:, :ck] = jnp.dot(h, wu_ref[:, g0:g0 + ck].astype(BF16), preferred_element_type=F32)
        z_ref[:, ck:] = jnp.dot(h, wu_ref[:, v0:v0 + ck].astype(BF16), preferred_element_type=F32)
        gate = conv(z_ref, slice(0, ck), g0)
        val = conv(z_ref, slice(ck, 2 * ck), v0)
        act_scr[:, g0:g0 + ck] = (_silu(gate) * val).astype(BF16)
    y = jnp.dot(act_scr[...], wd_ref[...].astype(BF16), preferred_element_type=F32)
    o_ref[...] = _residual(x_ref[...], y, g3_ref[...], mod_ref, 5)


def _conv_ffn(x, mod, gain2, gain3, w_up, conv_w, w_down, seq_len, rows_per_mod):
    t = x.shape[0]
    tm = min(FFN_ROW_TILE, seq_len)
    nblk8 = t // SUBLANES
    r8 = tm // SUBLANES
    resident = lambda shape: pl.BlockSpec(shape, lambda i: (0, 0), pipeline_mode=pl.Buffered(1))
    return pl.pallas_call(
        functools.partial(_ffn_kernel, tm=tm, seq_len=seq_len),
        grid=(t // tm,),
        in_specs=[
            pl.BlockSpec((tm, D_MODEL), lambda i: (i, 0)),
            pl.BlockSpec((SUBLANES, D_MODEL), lambda i: (jnp.maximum(i * r8 - 1, 0), 0)),
            pl.BlockSpec((SUBLANES, D_MODEL), lambda i: (jnp.minimum((i + 1) * r8, nblk8 - 1), 0)),
            _mod_spec(tm, rows_per_mod),
            _const_spec((1, D_MODEL)),
            _const_spec((1, D_MODEL)),
            resident((D_MODEL, 2 * FFN_DIM)),
            resident((3, 2 * FFN_DIM)),
            resident((FFN_DIM, D_MODEL)),
        ],
        out_specs=pl.BlockSpec((tm, D_MODEL), lambda i: (i, 0)),
        out_shape=jax.ShapeDtypeStruct((t, D_MODEL), F32),
        scratch_shapes=[pltpu.VMEM((tm + 2 * SUBLANES, D_MODEL), BF16),
                        pltpu.VMEM((2, tm + 2 * SUBLANES, 2 * FFN_CHUNK), F32),
                        pltpu.VMEM((tm, FFN_DIM), BF16)],
        compiler_params=_cparams("parallel"),
        name="conv_ffn",
    )(x, x, x, mod, gain2, gain3, w_up, conv_w, w_down)


ROPE_HALF = DIFF_DH // 4
LOG2E = 1.4426950408889634


def _rope(x, cos, sin_signed):
    lane = lax.broadcasted_iota(jnp.int32, x.shape, 1)
    first = (lane % (2 * ROPE_HALF)) < ROPE_HALF
    rot = jnp.where(first, pltpu.roll(x, LANES - ROPE_HALF, 1), pltpu.roll(x, ROPE_HALF, 1))
    return x * cos + rot * sin_signed


def _diff_attn_body(q_ref, k_ref, v_ref, ck_ref, cv_ref, cos_ref, sin_ref, lam_ref, subln_ref,
                    o_ref, k_scr, vt_scr, s_scr, *, n_cache, tq, lam_init):
    qi = pl.program_id(2)
    seq_len = k_ref.shape[0]
    n_keys = n_cache + seq_len
    rope = cos_ref is not None
    tr = 512 if seq_len % 512 == 0 else seq_len

    @pl.when(qi == 0)
    def _():
        k = k_ref[...]
        if rope:
            k = _rope(k, cos_ref[...], sin_ref[...])
        if n_cache:
            k_scr[0:n_cache] = ck_ref[0].astype(BF16)
            vt_scr[:, 0:n_cache] = cv_ref[0].T.astype(BF16)
        k_scr[n_cache:] = k.astype(BF16)
        for r in range(0, seq_len, tr):
            vt_scr[:, n_cache + r:n_cache + r + tr] = v_ref[r:r + tr, :].T.astype(BF16)

    n_tiles = q_ref.shape[0] // tq
    kc = ATT_KEY_CHUNK
    n_chunks = n_keys // kc
    part = lambda x, op: functools.reduce(op, [x[r:r + SUBLANES] for r in range(0, kc, SUBLANES)])
    lf = lam_ref[...]
    lam = (jnp.exp(jnp.sum(lf[0:1] * lf[1:2], axis=1, keepdims=True))
           - jnp.exp(jnp.sum(lf[2:3] * lf[3:4], axis=1, keepdims=True)) + lam_init)

    def tile_rows(t):
        start = t * tq
        return pl.ds(start if isinstance(start, int) else pl.multiple_of(start, tq), tq)

    def load_q(t):
        q = q_ref[tile_rows(t), :]
        if rope:
            g0 = pl.multiple_of((qi * n_tiles + t) * tq, tq)
            q = _rope(q, cos_ref[pl.ds(g0, tq), :], sin_ref[pl.ds(g0, tq), :])
        q = q * (DIFF_DH ** -0.5 * LOG2E)
        lane = lax.broadcasted_iota(jnp.int32, q.shape, 1)
        return (jnp.where(lane < DIFF_DH, q, 0.0).astype(BF16), jnp.where(lane >= DIFF_DH, q, 0.0).astype(BF16))

    def scores(slot, j, qz, m):
        kch = k_scr[j * kc:(j + 1) * kc, :]
        for c in range(2):
            s = lax.dot_general(kch, qz[c], (((1,), (1,)), ((), ())), preferred_element_type=F32)
            s_scr[slot, c, j * kc:(j + 1) * kc, :] = s
            pm = part(s, jnp.maximum)
            m[c] = pm if m[c] is None else jnp.maximum(m[c], pm)

    def values(slot, j, m, l, acc):
        vt = vt_scr[:, j * kc:(j + 1) * kc]
        for c in range(2):
            p = jnp.exp2(s_scr[slot, c, j * kc:(j + 1) * kc, :] - m[c])
            ps = part(p, jnp.add)
            pv = jnp.dot(vt, p.astype(BF16), preferred_element_type=F32)
            l[c] = ps if l[c] is None else l[c] + ps
            acc[c] = pv if acc[c] is None else acc[c] + pv

    def finish(t, l, acc):
        l = [jnp.sum(x, axis=0, keepdims=True) for x in l]
        o = acc[0] * (1.0 / l[0]) - lam * (acc[1] * (1.0 / l[1]))
        ms = jnp.mean(o * o, axis=0, keepdims=True)
        y = (o * lax.rsqrt(ms + EPS) * subln_ref[...]) * (1.0 - lam_init)
        o_ref[tile_rows(t), :] = y.T

    def phase(t, slot, m_prev, do_scores, do_values):
        qz = load_q(t) if do_scores else None
        m, l, acc = [None, None], [None, None], [None, None]
        for j in range(n_chunks):
            if do_scores:
                scores(slot, j, qz, m)
            if do_values:
                values(1 - slot, j, m_prev, l, acc)
        if do_values:
            finish(t - 1, l, acc)
        if do_scores:
            return tuple(jnp.max(x, axis=0, keepdims=True) for x in m)
        return m_prev

    def phase_pair(i, m_prev):
        t = 2 * i + 1
        return phase(t + 1, 0, phase(t, 1, m_prev, True, True), True, True)

    m_prev = phase(0, 0, None, True, False)
    n_pairs = (n_tiles - 1) // 2
    if n_pairs:
        m_prev = lax.fori_loop(0, n_pairs, phase_pair, m_prev)
    if (n_tiles - 1) % 2:
        m_prev = phase(n_tiles - 1, 1, m_prev, True, True)
    phase(n_tiles, n_tiles % 2, m_prev, False, True)


def _diff_attn_cached_kernel(q_ref, k_ref, v_ref, ck_ref, cv_ref, cos_ref, sin_ref, lam_ref, subln_ref,
                             o_ref, k_scr, vt_scr, s_scr, **kw):
    _diff_attn_body(q_ref, k_ref, v_ref, ck_ref, cv_ref, cos_ref, sin_ref, lam_ref, subln_ref,
                    o_ref, k_scr, vt_scr, s_scr, **kw)


def _diff_attn_context_kernel(q_ref, k_ref, v_ref, lam_ref, subln_ref, o_ref, k_scr, vt_scr, s_scr, **kw):
    _diff_attn_body(q_ref, k_ref, v_ref, None, None, None, None, lam_ref, subln_ref,
                    o_ref, k_scr, vt_scr, s_scr, **kw)


def _diff_attention(qkv, lam_vecs, subln, lam_init, nb, seq_len, cache=None, rope_tables=None):
    tq = ATT_QUERY_TILE
    n_tiles = min(ATT_TILES_PER_STEP, seq_len // tq)
    tqs = n_tiles * tq
    nq = seq_len // tqs
    width = 2 * DIFF_DH
    n_cache = 0 if cache is None else cache[0].shape[1]
    n_keys = n_cache + seq_len
    in_specs = [pl.BlockSpec((tqs, width), lambda b, h, i: (b * nq + i, h)),
                pl.BlockSpec((seq_len, width), lambda b, h, i: (b, DIFF_HEADS + h)),
                pl.BlockSpec((seq_len, width), lambda b, h, i: (b, 2 * DIFF_HEADS + h))]
    args = [qkv, qkv, qkv]
    if cache is not None:
        in_specs += [pl.BlockSpec((1, n_cache, width), lambda b, h, i: (b, 0, h))] * 2
        in_specs += [pl.BlockSpec((seq_len, width), lambda b, h, i: (0, 0))] * 2
        args += [cache[0], cache[1], rope_tables[0], rope_tables[1]]
        kern = _diff_attn_cached_kernel
    else:
        kern = _diff_attn_context_kernel
    in_specs += [pl.BlockSpec((4, DIFF_DH), lambda b, h, i: (0, 0)),
                 pl.BlockSpec((width, 1), lambda b, h, i: (0, 0))]
    args += [lam_vecs, subln.reshape(width, 1)]
    return pl.pallas_call(
        functools.partial(kern, n_cache=n_cache, tq=tq, lam_init=lam_init),
        grid=(nb, DIFF_HEADS, nq),
        in_specs=in_specs,
        out_specs=pl.BlockSpec((tqs, width), lambda b, h, i: (b * nq + i, h)),
        out_shape=jax.ShapeDtypeStruct((nb * seq_len, D_MODEL), F32),
        scratch_shapes=[pltpu.VMEM((n_keys, width), BF16), pltpu.VMEM((width, n_keys), BF16),
                        pltpu.VMEM((min(n_tiles, 2), 2, n_keys, tq), F32)],
        compiler_params=_cparams("parallel", "parallel", "arbitrary"),
        name="diff_attention",
    )(*args)


def _rope_tables(n_tokens):
    rows = (jnp.arange(n_tokens, dtype=jnp.int32) // GRID_W).astype(F32)
    cols = (jnp.arange(n_tokens, dtype=jnp.int32) % GRID_W).astype(F32)
    n_freq = DIFF_DH // 4
    inv_freq = ROPE_BASE ** (-jnp.arange(n_freq, dtype=F32) / n_freq)
    ang_r = rows[:, None] * inv_freq
    ang_c = cols[:, None] * inv_freq
    ang = jnp.concatenate([ang_r, ang_r, ang_c, ang_c] * 2, axis=-1)
    sign = jnp.where((jnp.arange(2 * DIFF_DH) % (2 * ROPE_HALF)) < ROPE_HALF, -1.0, 1.0).astype(F32)
    return jnp.cos(ang), jnp.sin(ang) * sign


GDN_QKV = 3 * GDN_HEADS * GDN_DK
GDN_PROJ = GDN_QKV + D_MODEL + LANES
GDN_SEQS_PER_STEP = 2


def _cumsum_rows(x, reverse):
    n = x.shape[0]
    row = lax.broadcasted_iota(jnp.int32, x.shape, 0)
    k = 1
    while k < n:
        if reverse:
            x = x + jnp.where(row < n - k, pltpu.roll(x, n - k, 0), 0.0)
        else:
            x = x + jnp.where(row >= k, pltpu.roll(x, k, 0), 0.0)
        k *= 2
    return x


def _gdn_units(units, eye, blocks):
    c = GDN_CHUNK
    for n in units:
        n["decay"] = jnp.where(n["incl"], jnp.exp(jnp.where(n["incl"], n["gcol"] - n["grow"], 0.0)), 0.0)
        n["kb"] = n["k"] * n["bcol"]
        n["kt"] = n["k"].T
    for n in units:
        a = _dot(jnp.concatenate([n["kb"], n["q"]], axis=0), n["kt"])
        n["lower"] = jnp.where(n["strict"], a[:c] * n["decay"], 0.0)
        n["intra"] = a[c:] * n["decay"]
        n["t_off"] = -jnp.where(blocks[0], n["lower"], 0.0)
    for blk in blocks[1:]:
        for n in units:
            n["p"] = _dot(eye + n["t_off"], jnp.where(blk, n["lower"], 0.0))
        for n in units:
            n["t_off"] = n["t_off"] - _dot(n["p"], eye + n["t_off"])
    for n in units:
        egc = jnp.exp(n["gcol"])
        rhs = jnp.concatenate([n["v"] * n["bcol"], n["kb"] * egc], axis=1)
        sol = rhs + _dot(n["t_off"], rhs)
        n["u"] = sol[:, :GDN_DV]
        n["wq"] = jnp.concatenate([sol[:, GDN_DV:], n["q"] * egc], axis=0)
    for n in units:
        ws = _dot(n["wq"], n["s"])
        n["v_new"] = n["u"] - ws[:c]
        n["qs"] = ws[c:]
        n["ikt"] = jnp.concatenate([n["intra"], n["kt"] * jnp.exp(n["gtot"] - n["grow"])], axis=0)
    outs = []
    for n in units:
        r = _dot(n["ikt"], n["v_new"])
        outs.append((n["qs"] + r[:c], n["s"] * jnp.exp(n["gtot"]) + r[c:]))
    return outs


def _gdn_prep_kernel(m_ref, p_ref, n_ref, ab_ref, cw_ref, alog_ref, dtb_ref, qkv_ref, gate_ref, ext_scr,
                     *, tm, seq_len):
    i = pl.program_id(0)
    keep_prev = jnp.where((i * tm) % seq_len == 0, 0.0, 1.0)
    keep_next = jnp.where(((i + 1) * tm) % seq_len == 0, 0.0, 1.0)
    ext_scr[0:SUBLANES] = p_ref[...] * keep_prev
    ext_scr[SUBLANES:SUBLANES + tm] = m_ref[...]
    ext_scr[SUBLANES + tm:] = n_ref[...] * keep_next
    ones = jnp.ones((GDN_DK, GDN_DK), BF16)
    for j in range(3 * GDN_HEADS):
        cols = slice(j * GDN_DK, (j + 1) * GDN_DK)
        cw = cw_ref[:, cols]
        x = _silu(cw[0:1] * ext_scr[SUBLANES - 1:SUBLANES - 1 + tm, cols]
                  + cw[1:2] * ext_scr[SUBLANES:SUBLANES + tm, cols]
                  + cw[2:3] * ext_scr[SUBLANES + 1:SUBLANES + 1 + tm, cols])
        if j < 2 * GDN_HEADS:
            sq = x * x
            hi = sq.astype(BF16)
            lo = (sq - hi.astype(F32)).astype(BF16)
            ssq = jnp.dot(hi, ones, preferred_element_type=F32) + jnp.dot(lo, ones, preferred_element_type=F32)
            x = x * lax.rsqrt(ssq + EPS)
            if j < GDN_HEADS:
                x = x * (GDN_DK ** -0.5)
        qkv_ref[:, cols] = x
    ab = ab_ref[...]
    z = ab + dtb_ref[...]
    g = -jnp.exp(alog_ref[...]) * (jnp.maximum(z, 0.0) + jnp.log(1.0 + jnp.exp(-jnp.abs(z))))
    lane = lax.broadcasted_iota(jnp.int32, ab.shape, 1)
    gate_ref[...] = jnp.where(lane < 2 * GDN_HEADS, g, jax.nn.sigmoid(ab))


def _gdn_prep(proj, conv_w, a_log, dt_bias, seq_len):
    t = proj.shape[0]
    tm = min(FFN_ROW_TILE, seq_len)
    r8 = tm // SUBLANES
    nblk8 = t // SUBLANES
    ab_col = (GDN_QKV + D_MODEL) // LANES
    pad = lambda p: jnp.pad(p.reshape(1, 2 * GDN_HEADS), ((0, 0), (0, LANES - 2 * GDN_HEADS)))
    return pl.pallas_call(
        functools.partial(_gdn_prep_kernel, tm=tm, seq_len=seq_len),
        grid=(t // tm,),
        in_specs=[pl.BlockSpec((tm, GDN_QKV), lambda i: (i, 0)),
                  pl.BlockSpec((SUBLANES, GDN_QKV), lambda i: (jnp.maximum(i * r8 - 1, 0), 0)),
                  pl.BlockSpec((SUBLANES, GDN_QKV), lambda i: (jnp.minimum((i + 1) * r8, nblk8 - 1), 0)),
                  pl.BlockSpec((tm, LANES), lambda i: (i, ab_col)),
                  _const_spec((3, GDN_QKV)), _const_spec((1, LANES)), _const_spec((1, LANES))],
        out_specs=[pl.BlockSpec((tm, GDN_QKV), lambda i: (i, 0)), pl.BlockSpec((tm, LANES), lambda i: (i, 0))],
        out_shape=[jax.ShapeDtypeStruct((t, GDN_QKV), F32), jax.ShapeDtypeStruct((t, LANES), F32)],
        scratch_shapes=[pltpu.VMEM((tm + 2 * SUBLANES, GDN_QKV), F32)],
        compiler_params=_cparams("parallel"),
        name="gdn_prep",
    )(proj, proj, proj, proj, conv_w, pad(a_log), pad(dt_bias))


def _gdn_kernel(fq_ref, fg_ref, bq_ref, bg_ref, *rest, has_s0):
    if has_s0:
        s0_ref, of_ref, ob_ref, s_ref = rest
    else:
        of_ref, ob_ref, s_ref = rest
    c = pl.program_id(1)

    @pl.when(c == 0)
    def _():
        if has_s0:
            s_ref[...] = s0_ref[...]
        else:
            s_ref[...] = jnp.zeros_like(s_ref)

    ri = lax.broadcasted_iota(jnp.int32, (GDN_CHUNK, GDN_CHUNK), 0)
    ci = lax.broadcasted_iota(jnp.int32, (GDN_CHUNK, GDN_CHUNK), 1)
    eye = jnp.where(ri == ci, 1.0, 0.0)
    blocks = []
    size = 1
    while size < GDN_CHUNK:
        blocks.append((ri // (2 * size) == ci // (2 * size)) & (ri // size != ci // size))
        size *= 2

    units = []
    for b in range(GDN_SEQS_PER_STEP):
        for d, (q_ref, g_ref) in enumerate(((fq_ref, fg_ref), (bq_ref, bg_ref))):
            gates = g_ref[b]
            gc = _cumsum_rows(gates, reverse=(d == 1))
            gct = jnp.concatenate([gc, gc], axis=0).T
            if d == 0:
                incl, strict, last = ri >= ci, ri > ci, GDN_CHUNK - 1
            else:
                incl, strict, last = ri <= ci, ri < ci, 0
            for h in range(GDN_HEADS):
                col = d * GDN_HEADS + h
                units.append(dict(
                    q=q_ref[b, :, h * GDN_DK:(h + 1) * GDN_DK],
                    k=q_ref[b, :, (GDN_HEADS + h) * GDN_DK:(GDN_HEADS + h + 1) * GDN_DK],
                    v=q_ref[b, :, (2 * GDN_HEADS + h) * GDN_DK:(2 * GDN_HEADS + h + 1) * GDN_DK],
                    gcol=gc[:, col:col + 1], grow=gct[col:col + 1, 0:GDN_CHUNK],
                    bcol=gates[:, 2 * GDN_HEADS + col:2 * GDN_HEADS + col + 1],
                    gtot=gc[last:last + 1, col:col + 1], s=s_ref[b, d, h], incl=incl, strict=strict))

    outs = iter(_gdn_units(units, eye, blocks))
    for b in range(GDN_SEQS_PER_STEP):
        for d, o_ref in enumerate((of_ref, ob_ref)):
            for h in range(GDN_HEADS):
                o, s_new = next(outs)
                o_ref[b, :, h * GDN_DV:(h + 1) * GDN_DV] = o
                s_ref[b, d, h] = s_new


def _gdn_scan(qkv, gates, nb, seq_len, s0=None):
    n = seq_len // GDN_CHUNK
    g = GDN_SEQS_PER_STEP
    hv = GDN_HEADS * GDN_DV

    def chunk_specs(chunk_of):
        return [pl.BlockSpec((g, GDN_CHUNK, GDN_QKV), lambda b, c: (b, chunk_of(c), 0)),
                pl.BlockSpec((g, GDN_CHUNK, LANES), lambda b, c: (b, chunk_of(c), 0))]

    state_spec = pl.BlockSpec((g, 2, GDN_HEADS, GDN_DK, GDN_DV), lambda b, c: (b, 0, 0, 0, 0))
    in_specs = chunk_specs(lambda c: c) + chunk_specs(lambda c: n - 1 - c)
    qkv = qkv.reshape(nb, seq_len, GDN_QKV)
    gates = gates.reshape(nb, seq_len, LANES)
    args = [qkv, gates, qkv, gates]
    if s0 is not None:
        in_specs.append(state_spec)
        args.append(s0)
    o_f, o_b, s_fin = pl.pallas_call(
        functools.partial(_gdn_kernel, has_s0=s0 is not None),
        grid=(nb // g, n),
        in_specs=in_specs,
        out_specs=[pl.BlockSpec((g, GDN_CHUNK, hv), lambda b, c: (b, c, 0)),
                   pl.BlockSpec((g, GDN_CHUNK, hv), lambda b, c: (b, n - 1 - c, 0)),
                   state_spec],
        out_shape=[jax.ShapeDtypeStruct((nb, seq_len, hv), F32), jax.ShapeDtypeStruct((nb, seq_len, hv), F32),
                   jax.ShapeDtypeStruct((nb, 2, GDN_HEADS, GDN_DK, GDN_DV), F32)],
        compiler_params=_cparams("parallel", "arbitrary"),
        name="gdn_scan",
    )(*args)
    return o_f.reshape(nb * seq_len, hv), o_b.reshape(nb * seq_len, hv), s_fin


def _gdn_mixer(x, mod, gains, w_cat, conv_w, a_log, dt_bias, head_gain, w_out, nb, seq_len, rows_per_mod,
               s0=None):
    proj = _norm_linear(x, mod, gains[0][None], w_cat, rows_per_mod, LANES * 11)
    qkv, gates = _gdn_prep(proj, conv_w, a_log, dt_bias, seq_len)
    o_f, o_b, s_fin = _gdn_scan(qkv, gates, nb, seq_len, s0)
    x = _gdn_out(x, o_f, o_b, proj, head_gain[None], mod, gains[1][None], w_out, rows_per_mod)
    return x, s_fin


def _diff_mixer(x, mod, gains, w_qkv, lam_vecs, subln, w_out, lam_init, nb, seq_len, rows_per_mod,
                cache=None, rope_tables=None):
    qkv = _norm_linear(x, mod, gains[0][None], w_qkv, rows_per_mod, D_MODEL)
    heads = _diff_attention(qkv, lam_vecs, subln, lam_init, nb, seq_len, cache, rope_tables)
    x = _plain_out(x, heads, mod, gains[1][None], w_out, rows_per_mod)
    return x, qkv


S5_IN = S5_CHUNK * S5_GROUP_CH
S5_BITS = 5


def _cmul(ar, ai, br, bi):
    return ar * br - ai * bi, ar * bi + ai * br


def _discretize(a_re, a_im, log_dt):
    dt = jnp.exp(log_dt)
    mag = jnp.exp(dt * a_re)
    abar_re = mag * jnp.cos(dt * a_im)
    abar_im = mag * jnp.sin(dt * a_im)
    den = a_re * a_re + a_im * a_im
    f_re = ((abar_re - 1.0) * a_re + abar_im * a_im) / den
    f_im = (abar_im * a_re - (abar_re - 1.0) * a_im) / den
    return abar_re, abar_im, f_re, f_im


def _cpow(ar, ai, expo):
    pr = jnp.ones(expo.shape, F32)
    pi = jnp.zeros(expo.shape, F32)
    for b in range(S5_BITS):
        bit = ((expo >> b) & 1) == 1
        pr, pi = _cmul(pr, pi, jnp.where(bit, ar, 1.0), jnp.where(bit, ai, 0.0))
        if b + 1 < S5_BITS:
            ar, ai = _cmul(ar, ai, ar, ai)
    return pr, pi


def _s5_ops_kernel(arow_ref, acol_ref, bpc_ref, bt_ref, ccat_ref, csep_ref, mc_ref, ef_ref, a16_ref):
    arow = arow_ref[0]
    ar, ai, fr, fi = _discretize(arow[0:1], arow[1:2], arow[2:3])
    row = lax.broadcasted_iota(jnp.int32, (S5_IN, LANES), 0)
    lane = lax.broadcasted_iota(jnp.int32, (S5_IN, LANES), 1)
    tok = row // S5_GROUP_CH
    fwd = lane < S5_STATE
    tile = lambda x: jnp.concatenate([x] * S5_CHUNK, axis=0)

    pr, pi = _cpow(ar, ai, jnp.where(fwd, S5_CHUNK - 1 - tok, tok))
    bt = bt_ref[0]
    bbr, bbi = _cmul(fr, fi, bt[0], bt[1])
    er, ei = _cmul(pr, pi, tile(bbr), tile(bbi))
    ef_ref[0, 0] = er
    ef_ref[0, 1] = ei
    pr, pi = _cpow(ar, ai, jnp.where(fwd, tok + 1, S5_CHUNK - tok))
    cc = ccat_ref[0]
    gr, gi = _cmul(pr, pi, tile(cc[0]), tile(cc[1]))
    ef_ref[0, 2] = gr
    ef_ref[0, 3] = -gi
    p16r, p16i = _cpow(ar, ai, jnp.full((1, LANES), S5_CHUNK, jnp.int32))
    a16_ref[0, 0:1] = p16r
    a16_ref[0, 1:2] = p16i

    acol = acol_ref[0]
    lane_w = lax.broadcasted_iota(jnp.int32, (S5_GROUP_CH, S5_IN), 1)
    place = jnp.where(lane_w % S5_GROUP_CH == lax.broadcasted_iota(jnp.int32, (S5_GROUP_CH, S5_IN), 0), 1.0, 0.0)
    lag = lax.broadcasted_iota(jnp.int32, (S5_STATE, S5_IN), 1) // S5_GROUP_CH
    kw = []
    for d in range(2):
        cr, ci, gr_, gi_ = _discretize(acol[:, 3 * d:3 * d + 1], acol[:, 3 * d + 1:3 * d + 2],
                                       acol[:, 3 * d + 2:3 * d + 3])
        bbr, bbi = _cmul(gr_, gi_, bpc_ref[0, 2 * d], bpc_ref[0, 2 * d + 1])
        pr, pi = _cpow(jnp.broadcast_to(cr, lag.shape), jnp.broadcast_to(ci, lag.shape),
                       S5_CHUNK - 1 - lag if d == 0 else lag)
        hr, hi = _cmul(pr, pi, _dot_f32(bbr, place), _dot_f32(bbi, place))
        kw.append(_dot_f32(csep_ref[0, 2 * d], hr) - _dot_f32(csep_ref[0, 2 * d + 1], hi))
    for t in range(S5_CHUNK):
        sh = (S5_CHUNK - 1 - t) * S5_GROUP_CH
        m_f = jnp.where(lane_w < (t + 1) * S5_GROUP_CH, pltpu.roll(kw[0], (S5_IN - sh) % S5_IN, 1), 0.0)
        m_b = jnp.where(lane_w >= t * S5_GROUP_CH, pltpu.roll(kw[1], t * S5_GROUP_CH, 1), 0.0)
        mc_ref[0, t * S5_GROUP_CH:(t + 1) * S5_GROUP_CH, :] = m_f + m_b


def _s5_operators(a_re, a_im, log_dt, b_re, b_im, c_re, c_im):
    g = S5_GROUPS
    cat = lambda x: jnp.concatenate([x[0], x[1]], axis=-1)
    dt_row = jnp.broadcast_to(log_dt[:, :, None], (2, g, S5_STATE))
    arow = jnp.stack([cat(a_re), cat(a_im), cat(dt_row)], axis=1)
    acol = jnp.stack([a_re[0], a_im[0], dt_row[0], a_re[1], a_im[1], dt_row[1]], axis=-1)
    bpc = jnp.stack([b_re[0], b_im[0], b_re[1], b_im[1]], axis=1)
    swap = lambda x: jnp.swapaxes(x, -1, -2)
    bt = jnp.stack([cat(swap(b_re)), cat(swap(b_im))], axis=1)
    ccat = jnp.stack([cat(c_re), cat(c_im)], axis=1)
    csep = jnp.stack([c_re[0], c_im[0], c_re[1], c_im[1]], axis=1)
    spec = lambda *s: pl.BlockSpec((1,) + s, lambda i: (i,) + (0,) * len(s))
    return pl.pallas_call(
        _s5_ops_kernel,
        grid=(g,),
        in_specs=[spec(3, LANES), spec(S5_STATE, 6), spec(4, S5_STATE, S5_GROUP_CH),
                  spec(2, S5_GROUP_CH, LANES), spec(2, S5_GROUP_CH, LANES), spec(4, S5_GROUP_CH, S5_STATE)],
        out_specs=[spec(S5_IN, S5_IN), spec(4, S5_IN, LANES), spec(2, LANES)],
        out_shape=[jax.ShapeDtypeStruct((g, S5_IN, S5_IN), F32),
                   jax.ShapeDtypeStruct((g, 4, S5_IN, LANES), F32),
                   jax.ShapeDtypeStruct((g, 2, LANES), F32)],
        compiler_params=_cparams("parallel"),
        name="s5_operators",
    )(arow, acol, bpc, bt, ccat, csep)


def _s5_scan_kernel(u_ref, mc_ref, ef_ref, a16_ref, *rest, nb, has_x0):
    if has_x0:
        x0r_ref, x0i_ref, y_ref, fr_ref, fi_ref, xr_scr, xi_scr = rest
    else:
        y_ref, fr_ref, fi_ref, xr_scr, xi_scr = rest
    u = u_ref[0].astype(BF16)
    rows = u.shape[0]
    nc = rows // nb
    vr = jnp.dot(u, ef_ref[0, 0].astype(BF16), preferred_element_type=F32)
    vi = jnp.dot(u, ef_ref[0, 1].astype(BF16), preferred_element_type=F32)
    row_c = lax.broadcasted_iota(jnp.int32, (rows, LANES), 0) % nc
    fwd = lax.broadcasted_iota(jnp.int32, (rows, LANES), 1) < S5_STATE
    pr = a16_ref[0, 0:1]
    pi = a16_ref[0, 1:2]
    if has_x0:
        expand = lambda x: jnp.broadcast_to(x[:, None, :], (nb, nc, LANES)).reshape(rows, LANES)
        x0r, x0i = expand(x0r_ref[0]), expand(x0i_ref[0])
        first = row_c == jnp.where(fwd, 0, nc - 1)
        ir, ii = _cmul(pr, pi, x0r, x0i)
        vr = vr + jnp.where(first, ir, 0.0)
        vi = vi + jnp.where(first, ii, 0.0)

    def shifted(x, k):
        down = jnp.where(row_c >= k, pltpu.roll(x, k, 0), 0.0)
        up = jnp.where(row_c < nc - k, pltpu.roll(x, rows - k, 0), 0.0)
        return jnp.where(fwd, down, up)

    k = 1
    while k < nc:
        sr, si = _cmul(pr, pi, shifted(vr, k), shifted(vi, k))
        vr, vi = vr + sr, vi + si
        pr, pi = _cmul(pr, pi, pr, pi)
        k *= 2
    er, ei = shifted(vr, 1), shifted(vi, 1)
    if has_x0:
        er = jnp.where(first, x0r, er)
        ei = jnp.where(first, x0i, ei)
    y = lax.dot_general(u, mc_ref[0].astype(BF16), (((1,), (1,)), ((), ())), preferred_element_type=F32)
    y = y + _dot_nt(er, ef_ref[0, 2]) + _dot_nt(ei, ef_ref[0, 3])
    y_ref[0] = y
    xr_scr[...] = vr
    xi_scr[...] = vi
    lane_b = lax.broadcasted_iota(jnp.int32, (nb, LANES), 1) < S5_STATE
    fr_ref[0] = jnp.where(lane_b, xr_scr[pl.ds(nc - 1, nb, stride=nc), :], xr_scr[pl.ds(0, nb, stride=nc), :])
    fi_ref[0] = jnp.where(lane_b, xi_scr[pl.ds(nc - 1, nb, stride=nc), :], xi_scr[pl.ds(0, nb, stride=nc), :])


def _s5_scan(u, ops, nb, x0=None):
    mc, ef, a16 = ops
    g, rows, _ = u.shape
    spec = lambda *s: pl.BlockSpec((1,) + s, lambda i: (i,) + (0,) * len(s))
    in_specs = [spec(rows, S5_IN), spec(S5_IN, S5_IN), spec(4, S5_IN, LANES), spec(2, LANES)]
    args = [u, mc, ef, a16]
    if x0 is not None:
        in_specs += [spec(nb, LANES), spec(nb, LANES)]
        args += list(x0)
    return pl.pallas_call(
        functools.partial(_s5_scan_kernel, nb=nb, has_x0=x0 is not None),
        grid=(g,),
        in_specs=in_specs,
        out_specs=[spec(rows, S5_IN), spec(nb, LANES), spec(nb, LANES)],
        out_shape=[jax.ShapeDtypeStruct((g, rows, S5_IN), F32),
                   jax.ShapeDtypeStruct((g, nb, LANES), F32),
                   jax.ShapeDtypeStruct((g, nb, LANES), F32)],
        scratch_shapes=[pltpu.VMEM((rows, LANES), F32), pltpu.VMEM((rows, LANES), F32)],
        compiler_params=_cparams("parallel"),
        name="s5_scan",
    )(*args)


def _to_chunk_major(h, nb, seq_len):
    nc = seq_len // S5_CHUNK
    h = h.reshape(nb * nc, S5_CHUNK, S5_GROUPS, S5_GROUP_CH)
    return jnp.transpose(h, (2, 0, 1, 3)).reshape(S5_GROUPS, nb * nc, S5_IN)


def _from_chunk_major(y, nb, seq_len):
    nc = seq_len // S5_CHUNK
    y = y.reshape(S5_GROUPS, nb * nc, S5_CHUNK, S5_GROUP_CH)
    return jnp.transpose(y, (1, 2, 0, 3)).reshape(nb * seq_len, D_MODEL)


def _s5_mixer(x, mod, gains, ops, d_skip, w_glu, nb, seq_len, rows_per_mod, cache=None):
    h = _norm_mod(x, mod, gains[0][None], rows_per_mod)
    x0 = None
    if cache is not None:
        x0 = [jnp.transpose(s, (2, 0, 1, 3)).reshape(S5_GROUPS, nb, 2 * S5_STATE) for s in cache]
    y, fin_re, fin_im = _s5_scan(_to_chunk_major(h.astype(BF16), nb, seq_len), ops, nb, x0)
    x = _s5_out(x, h, _from_chunk_major(y, nb, seq_len), d_skip[None], mod, gains[1][None], w_glu, rows_per_mod)
    unpack = lambda f: jnp.transpose(f.reshape(S5_GROUPS, nb, 2, S5_STATE), (1, 2, 0, 3))
    return x, (unpack(fin_re), unpack(fin_im))


def kernel(x_prompt, x_sample, state_l0_gdn, cache_l1_k, cache_l1_v, state_l2_s5_re, state_l2_s5_im, state_l3_gdn, c, c_ctx, w_mod, b_mod, norm_gain, w_ffn_up, ffn_conv, w_ffn_down, w_gdn_qkv, gdn_conv, w_gdn_gate, w_gdn_alpha, w_gdn_beta, gdn_a_log, gdn_dt_bias, gdn_norm, w_gdn_out, w_diff_qkv, diff_lam, diff_subln, w_diff_out, s5_a_re, s5_a_im, s5_log_dt, s5_b_re, s5_b_im, s5_c_re, s5_c_im, s5_d, w_s5_glu):
    nbp, lp = x_prompt.shape[:2]
    nbs, ls = x_sample.shape[:2]
    tp, ts = nbp * lp, nbs * ls
    pad_rows = -(nbs + 1) % SUBLANES
    cond = jnp.concatenate([c, c_ctx[None], jnp.zeros((pad_rows, D_MODEL), F32)], axis=0)
    mod_all = _modulation(cond, w_mod, b_mod)
    s5_ops = _s5_operators(s5_a_re[0], s5_a_im[0], s5_log_dt[0], s5_b_re[0], s5_b_im[0], s5_c_re[0], s5_c_im[0])
    rope = _rope_tables(ls)
    past = cache_l1_k.shape[1]
    attn_cache = (cache_l1_k.reshape(nbs, past, D_MODEL), cache_l1_v.reshape(nbs, past, D_MODEL))
    gdn_caches = (state_l0_gdn, state_l3_gdn)
    bf = lambda w: w.astype(BF16)

    xp = x_prompt.reshape(tp, D_MODEL)
    xs = x_sample.reshape(ts, D_MODEL)
    ctx_states = []
    for l in range(DEPTH):
        kind, j = l % 3, l // 3
        gains = norm_gain[l]
        mod_s = mod_all[l, :nbs][:, None, :]
        mod_p = mod_all[l, nbs:nbs + 1][:, None, :]
        if kind == 0:
            w_cat = jnp.concatenate([w_gdn_qkv[j], w_gdn_gate[j], w_gdn_alpha[j], w_gdn_beta[j],
                                     jnp.zeros((D_MODEL, LANES - 4 * GDN_HEADS), F32)], axis=1).astype(BF16)
            gdn = functools.partial(_gdn_mixer, gains=gains, w_cat=w_cat, conv_w=gdn_conv[j], a_log=gdn_a_log[j],
                                    dt_bias=gdn_dt_bias[j], head_gain=gdn_norm[j], w_out=bf(w_gdn_out[j]))
            xp, st = gdn(xp, mod_p, nb=nbp, seq_len=lp, rows_per_mod=tp)
            xs, _ = gdn(xs, mod_s, nb=nbs, seq_len=ls, rows_per_mod=ls, s0=gdn_caches[j])
            ctx_states.append(st)
        elif kind == 1:
            lam_init = 0.8 - 0.6 * math.exp(-0.3 * l)
            attn = functools.partial(_diff_mixer, gains=gains, w_qkv=bf(w_diff_qkv[j]), lam_vecs=diff_lam[j],
                                     subln=diff_subln[j], w_out=bf(w_diff_out[j]), lam_init=lam_init)
            xp, qkv = attn(xp, mod_p, nb=nbp, seq_len=lp, rows_per_mod=tp)
            xs, _ = attn(xs, mod_s, nb=nbs, seq_len=ls, rows_per_mod=ls, cache=attn_cache, rope_tables=rope)
            ctx_states.append((qkv[:, D_MODEL:2 * D_MODEL].reshape(nbp, lp, DIFF_HEADS, 2, DIFF_DH),
                               qkv[:, 2 * D_MODEL:].reshape(nbp, lp, DIFF_HEADS, 2 * DIFF_DH)))
        else:
            s5 = functools.partial(_s5_mixer, gains=gains, ops=s5_ops, d_skip=s5_d[j], w_glu=bf(w_s5_glu[j]))
            xp, st = s5(xp, mod_p, nb=nbp, seq_len=lp, rows_per_mod=tp)
            xs, _ = s5(xs, mod_s, nb=nbs, seq_len=ls, rows_per_mod=ls, cache=(state_l2_s5_re, state_l2_s5_im))
            ctx_states.append(st)
        ffn = functools.partial(_conv_ffn, gain2=gains[2][None], gain3=gains[3][None], w_up=bf(w_ffn_up[l]),
                                conv_w=ffn_conv[l], w_down=bf(w_ffn_down[l]))
        xp = ffn(xp, mod_p, seq_len=lp, rows_per_mod=tp)
        xs = ffn(xs, mod_s, seq_len=ls, rows_per_mod=ls)
    st0, (k1, v1), (s2_re, s2_im), st3 = ctx_states
    return (xp.reshape(nbp, lp, D_MODEL), xs.reshape(nbs, ls, D_MODEL), st0, k1, v1, s2_re, s2_im, st3)
```
